```python
import math
import jax
import jax.numpy as jnp
from jax import lax
import numpy as np

D_MODEL = 1024
BATCH = 16
SEQ = 4096
DEPTH = 2

CHUNK = 64
Q_BLOCK = 128
GMLP_BLOCK = 128

A_GROUPS = 4
A_GDIM = 64
A_WIDTH = A_GROUPS * A_GDIM
B_HEADS = 4
B_QK_DIM = 64
B_V_DIM = 2 * B_QK_DIM
B_WIDTH = B_HEADS * B_V_DIM
C_HEADS = 4
C_HEAD_DIM = 64
C_WIDTH = C_HEADS * C_HEAD_DIM
IDX_HEADS = 8
IDX_DIM = 32
TOPK_MAX = 256
MIX_WIDTH = A_WIDTH + B_WIDTH + C_WIDTH

FFN_HIDDEN = -(-8 * D_MODEL // (3 * 256)) * 256

OFF_A_U = 0
OFF_A_V = OFF_A_U + A_WIDTH
OFF_B_Q = OFF_A_V + A_WIDTH
OFF_B_K = OFF_B_Q + B_HEADS * 2 * B_QK_DIM
OFF_B_V = OFF_B_K + B_HEADS * 2 * B_QK_DIM
OFF_C_Q = OFF_B_V + B_WIDTH
OFF_C_K = OFF_C_Q + C_WIDTH
OFF_C_V = OFF_C_K + C_WIDTH
OFF_I_Q = OFF_C_V + C_WIDTH
OFF_I_K = OFF_I_Q + IDX_HEADS * IDX_DIM
OFF_I_W = OFF_I_K + IDX_DIM
IN_WIDTH = OFF_I_W + IDX_HEADS

DEEPNORM_ALPHA = (2 * DEPTH) ** 0.25
DEEPNORM_BETA = (8 * DEPTH) ** -0.25
LN_EPS = 1e-5

kernel_name = 'hybrid_gmlp_diffattn_dsa_deepnorm'


def layer_norm(x, g, b):
    xf = x.astype(jnp.float32)
    mu = jnp.mean(xf, axis=-1, keepdims=True)
    xc = xf - mu
    var = jnp.mean(xc * xc, axis=-1, keepdims=True)
    return (xc * lax.rsqrt(var + LN_EPS) * g.astype(jnp.float32) + b.astype(jnp.float32)).astype(x.dtype)


def rms_norm(x, g):
    xf = x.astype(jnp.float32)
    ms = jnp.mean(xf * xf, axis=-1, keepdims=True)
    return (xf * lax.rsqrt(ms + LN_EPS) * g.astype(jnp.float32)).astype(x.dtype)


def alibi_slopes(n):
    return jnp.asarray([2.0 ** (-8.0 * (h + 1) / n) for h in range(n)], dtype=jnp.float32)


def to_blocks(a, size):
    bn, s = a.shape[:2]
    return jnp.moveaxis(a.reshape((bn, s // size, size) + a.shape[2:]), 1, 0)


def from_blocks(a):
    n, bn, t = a.shape[:3]
    return jnp.moveaxis(a, 0, 1).reshape((bn, n * t) + a.shape[3:])


def gmlp_mixer(u, v, w_s, b_s, ln_g, ln_b):
    bn, s, _ = u.shape
    nblk = s // GMLP_BLOCK
    u = jax.nn.gelu(u)
    v = jax.nn.gelu(v).reshape(bn, nblk, GMLP_BLOCK, A_GROUPS, A_GDIM)
    v = layer_norm(v, ln_g, ln_b)
    pos_chunk = jnp.arange(GMLP_BLOCK) // CHUNK
    mask = pos_chunk[None, :] <= pos_chunk[:, None]
    w = jnp.where(mask[None], w_s, jnp.zeros_like(w_s))
    sg = jnp.einsum('gts,bnsgc->bntgc', w, v) + b_s.T[None, None, :, :, None]
    return u * sg.reshape(bn, s, A_WIDTH)


def diff_attention(q, k, v, lam, lam_init, subln_g):
    bn, s = q.shape[:2]
    nblk = s // Q_BLOCK
    scale = B_QK_DIM ** -0.5
    slopes = alibi_slopes(B_HEADS)
    kpos = jnp.arange(s)
    kchunk = kpos // CHUNK

    def block(args):
        qblk, i = args
        qpos = i * Q_BLOCK + jnp.arange(Q_BLOCK)
        allowed = kchunk[None, :] <= (qpos // CHUNK)[:, None]
        dist = jnp.abs(qpos[:, None] - kpos[None, :]).astype(jnp.float32)
        bias = -slopes[:, None, None] * dist[None]
        logits = jnp.einsum('bthmd,bshmd->bhmts', qblk, k).astype(jnp.float32) * scale
        logits = logits + bias[None, :, None]
        logits = jnp.where(allowed[None, None, None], logits, -jnp.inf)
        p = jax.nn.softmax(logits, axis=-1)
        attn = p[:, :, 0] - lam * p[:, :, 1]
        return jnp.einsum('bhts,bshe->bthe', attn.astype(v.dtype), v)

    o = from_blocks(lax.map(block, (to_blocks(q, Q_BLOCK), jnp.arange(nblk))))
    o = rms_norm(o, subln_g) * (1.0 - lam_init)
    return o.reshape(bn, s, B_WIDTH)


def dsa_attention(q, k, v, qi, ki, wi):
    bn, s = q.shape[:2]
    nblk = s // Q_BLOCK
    topk = min(TOPK_MAX, s // 4)
    scale = C_HEAD_DIM ** -0.5
    idx_scale = (IDX_HEADS ** -0.5) * (IDX_DIM ** -0.5)
    slopes = alibi_slopes(C_HEADS)
    kchunk = jnp.arange(s) // CHUNK

    def block(args):
        qblk, qiblk, wiblk, i = args
        qpos = i * Q_BLOCK + jnp.arange(Q_BLOCK)
        qchunk = qpos // CHUNK
        allowed = kchunk[None, :] <= qchunk[:, None]
        idx_logits = jnp.einsum('bthd,bsd->bths', qiblk, ki).astype(jnp.float32)
        score = jnp.einsum('bth,bths->bts', wiblk.astype(jnp.float32) * idx_scale, jax.nn.relu(idx_logits))
        score = jnp.where(allowed[None], score, -jnp.inf)
        _, sel = lax.top_k(score, topk)
        sel_ok = (sel // CHUNK) <= qchunk[None, :, None]
        k_sel = jax.vmap(lambda kb, ib: kb[ib])(k, sel)
        v_sel = jax.vmap(lambda vb, ib: vb[ib])(v, sel)
        logits = jnp.einsum('bthd,btkhd->bhtk', qblk, k_sel).astype(jnp.float32) * scale
        dist = jnp.abs(qpos[None, :, None] - sel).astype(jnp.float32)
        logits = logits - slopes[None, :, None, None] * dist[:, None]
        logits = jnp.where(sel_ok[:, None], logits, -jnp.inf)
        p = jax.nn.softmax(logits, axis=-1)
        return jnp.einsum('bhtk,btkhd->bthd', p.astype(v_sel.dtype), v_sel)

    xs = (to_blocks(q, Q_BLOCK), to_blocks(qi, Q_BLOCK), to_blocks(wi, Q_BLOCK), jnp.arange(nblk))
    o = from_blocks(lax.map(block, xs))
    return o.reshape(bn, s, C_WIDTH)


def hybrid_layer(x, layer_idx, w_in, w_s, b_s, a_ln_g, a_ln_b, lam_q1, lam_k1, lam_q2, lam_k2,
                 subln_g, w_out, ln1_g, ln1_b, w_gu, w_down, ln2_g, ln2_b):
    bn, s, _ = x.shape
    h = jnp.einsum('bsd,de->bse', x, w_in)
    a_u = h[..., OFF_A_U:OFF_A_V]
    a_v = h[..., OFF_A_V:OFF_B_Q]
    b_q = h[..., OFF_B_Q:OFF_B_K].reshape(bn, s, B_HEADS, 2, B_QK_DIM)
    b_k = h[..., OFF_B_K:OFF_B_V].reshape(bn, s, B_HEADS, 2, B_QK_DIM)
    b_v = h[..., OFF_B_V:OFF_C_Q].reshape(bn, s, B_HEADS, B_V_DIM)
    c_q = h[..., OFF_C_Q:OFF_C_K].reshape(bn, s, C_HEADS, C_HEAD_DIM)
    c_k = h[..., OFF_C_K:OFF_C_V].reshape(bn, s, C_HEADS, C_HEAD_DIM)
    c_v = h[..., OFF_C_V:OFF_I_Q].reshape(bn, s, C_HEADS, C_HEAD_DIM)
    i_q = h[..., OFF_I_Q:OFF_I_K].reshape(bn, s, IDX_HEADS, IDX_DIM)
    i_k = h[..., OFF_I_K:OFF_I_W]
    i_w = h[..., OFF_I_W:IN_WIDTH]

    out_a = gmlp_mixer(a_u, a_v, w_s, b_s, a_ln_g, a_ln_b)

    lam_init = 0.8 - 0.6 * math.exp(-0.3 * layer_idx)
    lam = (jnp.exp(jnp.sum(lam_q1.astype(jnp.float32) * lam_k1.astype(jnp.float32)))
           - jnp.exp(jnp.sum(lam_q2.astype(jnp.float32) * lam_k2.astype(jnp.float32))) + lam_init)
    out_b = diff_attention(b_q, b_k, b_v, lam, lam_init, subln_g)

    out_c = dsa_attention(c_q, c_k, c_v, i_q, i_k, i_w)

    mix = jnp.concatenate([out_a, out_b, out_c], axis=-1)
    x = layer_norm(DEEPNORM_ALPHA * x + jnp.einsum('bse,ed->bsd', mix, w_out), ln1_g, ln1_b)

    gu = jnp.einsum('bsd,df->bsf', x, w_gu)
    gate, up = gu[..., :FFN_HIDDEN], gu[..., FFN_HIDDEN:]
    ffn = jnp.einsum('bsf,fd->bsd', jax.nn.silu(gate) * up, w_down)
    return layer_norm(DEEPNORM_ALPHA * x + ffn, ln2_g, ln2_b)


def setup_inputs(seed: int = 0) -> dict:
    key = jax.random.key(seed)
    ks = jax.random.split(key, 20)
    f32 = jnp.float32
    nrm = lambda k, shape: jax.random.normal(k, shape, dtype=f32)
    return {
        'x': nrm(ks[0], (BATCH, SEQ, D_MODEL)),
        'w_in': nrm(ks[1], (DEPTH, D_MODEL, IN_WIDTH)) * D_MODEL ** -0.5,
        'gmlp_w_s': nrm(ks[2], (DEPTH, A_GROUPS, GMLP_BLOCK, GMLP_BLOCK)) * GMLP_BLOCK ** -0.5,
        'gmlp_b_s': 1.0 + 0.1 * nrm(ks[3], (DEPTH, A_GROUPS, GMLP_BLOCK)),
        'gmlp_ln_g': 1.0 + 0.05 * nrm(ks[4], (DEPTH, A_GROUPS, A_GDIM)),
        'gmlp_ln_b': 0.02 * nrm(ks[5], (DEPTH, A_GROUPS, A_GDIM)),
        'lam_q1': 0.1 * nrm(ks[6], (DEPTH, B_QK_DIM)),
        'lam_k1': 0.1 * nrm(ks[7], (DEPTH, B_QK_DIM)),
        'lam_q2': 0.1 * nrm(ks[8], (DEPTH, B_QK_DIM)),
        'lam_k2': 0.1 * nrm(ks[9], (DEPTH, B_QK_DIM)),
        'diff_subln_g': 1.0 + 0.05 * nrm(ks[10], (DEPTH, B_V_DIM)),
        'w_out': nrm(ks[11], (DEPTH, MIX_WIDTH, D_MODEL)) * MIX_WIDTH ** -0.5 * DEEPNORM_BETA,
        'ln1_g': 1.0 + 0.05 * nrm(ks[12], (DEPTH, D_MODEL)),
        'ln1_b': 0.02 * nrm(ks[13], (DEPTH, D_MODEL)),
        'w_gu': nrm(ks[14], (DEPTH, D_MODEL, 2 * FFN_HIDDEN)) * D_MODEL ** -0.5,
        'w_down': nrm(ks[15], (DEPTH, FFN_HIDDEN, D_MODEL)) * FFN_HIDDEN ** -0.5 * DEEPNORM_BETA,
        'ln2_g': 1.0 + 0.05 * nrm(ks[16], (DEPTH, D_MODEL)),
        'ln2_b': 0.02 * nrm(ks[17], (DEPTH, D_MODEL)),
    }


def reference(x, w_in, gmlp_w_s, gmlp_b_s, gmlp_ln_g, gmlp_ln_b, lam_q1, lam_k1, lam_q2, lam_k2,
              diff_subln_g, w_out, ln1_g, ln1_b, w_gu, w_down, ln2_g, ln2_b):
    for l in range(DEPTH):
        x = hybrid_layer(x, l, w_in[l], gmlp_w_s[l], gmlp_b_s[l], gmlp_ln_g[l], gmlp_ln_b[l],
                         lam_q1[l], lam_k1[l], lam_q2[l], lam_k2[l], diff_subln_g[l], w_out[l],
                         ln1_g[l], ln1_b[l], w_gu[l], w_down[l], ln2_g[l], ln2_b[l])
    return x
```

```python
import functools
import math

import jax
import jax.numpy as jnp
from jax import lax
from jax.experimental import pallas as pl
from jax.experimental.pallas import tpu as pltpu

CHUNK = 64
GMLP_BLOCK = 128
A_GROUPS, A_GDIM = 4, 64
A_WIDTH = A_GROUPS * A_GDIM
B_HEADS, B_QK_DIM = 4, 64
B_V_DIM = 2 * B_QK_DIM
B_WIDTH = B_HEADS * B_V_DIM
C_HEADS, C_HEAD_DIM = 4, 64
C_WIDTH = C_HEADS * C_HEAD_DIM
IDX_HEADS, IDX_DIM = 8, 32
TOPK_MAX = 256
LN_EPS = 1e-5

HB_BQ = 0
HB_BK = HB_BQ + B_WIDTH
HB_BV = HB_BK + B_WIDTH
HB_CQ = HB_BV + B_WIDTH
HB_CK = HB_CQ + C_WIDTH
HB_CV = HB_CK + C_WIDTH
HB_IQ = HB_CV + C_WIDTH
HB_IKW = HB_IQ + IDX_HEADS * IDX_DIM
LANES = 128
HB_WIDTH = HB_IKW + LANES
IKW_W_OFF = IDX_DIM

MXU_DTYPE = jnp.bfloat16
NEG_BIG = -1e30
INT_MIN = -(2 ** 31)

ROW_TILE = 512
DIFF_TQ = 256
DSA_TQ = 128
DSA_G = 256
VMEM_LIMIT = 56 * 1024 * 1024


def _dot(a, b):
    return jnp.dot(a, b, preferred_element_type=jnp.float32)


def _dot_nt(a, b):
    return lax.dot_general(a, b, (((1,), (1,)), ((), ())), preferred_element_type=jnp.float32)


def _layer_norm_rows(z, g, b):
    mu = jnp.mean(z, axis=-1, keepdims=True)
    zc = z - mu
    var = jnp.mean(zc * zc, axis=-1, keepdims=True)
    return zc * lax.rsqrt(var + LN_EPS) * g + b


def _inproj_kernel(x_ref, wa_ref, wb_ref, ha_ref, hb_ref, cvt_ref):
    xb = x_ref[...].astype(MXU_DTYPE)
    ha_ref[...] = _dot(xb, wa_ref[...])
    step = 512
    for c0 in range(0, HB_WIDTH, step):
        c1 = min(c0 + step, HB_WIDTH)
        hb_ref[:, c0:c1] = _dot(xb, wb_ref[:, c0:c1]).astype(hb_ref.dtype)
    cv = _dot(xb, wb_ref[:, HB_CV:HB_CV + C_WIDTH])
    for g in range(cvt_ref.shape[0]):
        cvt_ref[g] = cv[g * DSA_G:(g + 1) * DSA_G, :].T.astype(cvt_ref.dtype)


def _in_proj(x, wa, wb):
    bsz, s, d = x.shape
    tm = min(ROW_TILE, s)
    gpt = tm // DSA_G
    grid = (bsz, s // tm)
    return pl.pallas_call(
        _inproj_kernel,
        grid=grid,
        in_specs=[
            pl.BlockSpec((None, tm, d), lambda b, i: (b, i, 0)),
            pl.BlockSpec(wa.shape, lambda b, i: (0, 0)),
            pl.BlockSpec(wb.shape, lambda b, i: (0, 0)),
        ],
        out_specs=[
            pl.BlockSpec((None, tm, 2 * A_WIDTH), lambda b, i: (b, i, 0)),
            pl.BlockSpec((None, tm, HB_WIDTH), lambda b, i: (b, i, 0)),
            pl.BlockSpec((None, gpt, C_WIDTH, DSA_G), lambda b, i: (b, i, 0, 0)),
        ],
        out_shape=[
            jax.ShapeDtypeStruct((bsz, s, 2 * A_WIDTH), jnp.float32),
            jax.ShapeDtypeStruct((bsz, s, HB_WIDTH), MXU_DTYPE),
            jax.ShapeDtypeStruct((bsz, s // DSA_G, C_WIDTH, DSA_G), MXU_DTYPE),
        ],
        compiler_params=pltpu.CompilerParams(
            dimension_semantics=("parallel", "parallel"), vmem_limit_bytes=VMEM_LIMIT),
        name="in_proj",
    )(x, wa, wb)


def _gmlp_kernel(ha_ref, ws_ref, bs_ref, g_ref, b_ref, o_ref):
    nblk = ha_ref.shape[0] // GMLP_BLOCK
    r = lax.broadcasted_iota(jnp.int32, (GMLP_BLOCK, GMLP_BLOCK), 0) // CHUNK
    c = lax.broadcasted_iota(jnp.int32, (GMLP_BLOCK, GMLP_BLOCK), 1) // CHUNK
    causal = c <= r
    for blk in range(nblk):
        rows = slice(blk * GMLP_BLOCK, (blk + 1) * GMLP_BLOCK)
        u = jax.nn.gelu(ha_ref[rows, 0:A_WIDTH])
        v = jax.nn.gelu(ha_ref[rows, A_WIDTH:2 * A_WIDTH])
        outs = []
        for g in range(A_GROUPS):
            cols = slice(g * A_GDIM, (g + 1) * A_GDIM)
            vg = _layer_norm_rows(v[:, cols], g_ref[:, cols], b_ref[:, cols])
            w = jnp.where(causal, ws_ref[g], 0.0).astype(MXU_DTYPE)
            sg = _dot(w, vg.astype(MXU_DTYPE)) + bs_ref[:, g:g + 1]
            outs.append(u[:, cols] * sg)
        o_ref[rows, :] = jnp.concatenate(outs, axis=-1).astype(o_ref.dtype)


def _gmlp(ha, w_s, b_s_t, ln_g, ln_b):
    bsz, s, _ = ha.shape
    tm = min(ROW_TILE, s)
    return pl.pallas_call(
        _gmlp_kernel,
        grid=(bsz, s // tm),
        in_specs=[
            pl.BlockSpec((None, tm, 2 * A_WIDTH), lambda b, i: (b, i, 0)),
            pl.BlockSpec(w_s.shape, lambda b, i: (0, 0, 0)),
            pl.BlockSpec(b_s_t.shape, lambda b, i: (0, 0)),
            pl.BlockSpec(ln_g.shape, lambda b, i: (0, 0)),
            pl.BlockSpec(ln_b.shape, lambda b, i: (0, 0)),
        ],
        out_specs=pl.BlockSpec((None, tm, A_WIDTH), lambda b, i: (b, i, 0)),
        out_shape=jax.ShapeDtypeStruct((bsz, s, A_WIDTH), MXU_DTYPE),
        compiler_params=pltpu.CompilerParams(
            dimension_semantics=("parallel", "parallel"), vmem_limit_bytes=VMEM_LIMIT),
        name="gmlp",
    )(ha, w_s, b_s_t, ln_g, ln_b)


def _diff_attn_kernel(slopes_ref, q_ref, k_ref, v_ref, lam_ref, g_ref, o_ref, *, lam_init, tq):
    h = pl.program_id(1)
    i = pl.program_id(2)
    slope = slopes_ref[h]
    q = q_ref[...]
    lane = lax.broadcasted_iota(jnp.int32, q.shape, 1)
    q1 = jnp.where(lane < B_QK_DIM, q, jnp.zeros_like(q))
    q2 = jnp.where(lane >= B_QK_DIM, q, jnp.zeros_like(q))
    qpos = i * tq + lax.broadcasted_iota(jnp.int32, (tq, tq), 0)
    kloc = lax.broadcasted_iota(jnp.int32, (tq, tq), 1)

    def body(j, carry):
        m1, l1, a1, m2, l2, a2 = carry
        start = pl.multiple_of(j * tq, tq)
        kb = k_ref[pl.ds(start, tq), :]
        vb = v_ref[pl.ds(start, tq), :]
        kpos = j * tq + kloc
        bias = -slope * jnp.abs(qpos - kpos).astype(jnp.float32)
        bias = jnp.where((kpos // CHUNK) <= (qpos // CHUNK), bias, NEG_BIG)

        def upd(qm, m, l, a):
            s = _dot_nt(qm, kb) + bias
            m_new = jnp.maximum(m, jnp.max(s, axis=-1, keepdims=True))
            p = jnp.exp(s - m_new)
            alpha = jnp.exp(m - m_new)
            l_new = alpha * l + jnp.sum(p, axis=-1, keepdims=True)
            a_new = alpha * a + _dot(p.astype(MXU_DTYPE), vb)
            return m_new, l_new, a_new

        m1, l1, a1 = upd(q1, m1, l1, a1)
        m2, l2, a2 = upd(q2, m2, l2, a2)
        return m1, l1, a1, m2, l2, a2

    m0 = jnp.full((tq, 1), NEG_BIG, jnp.float32)
    l0 = jnp.zeros((tq, 1), jnp.float32)
    a0 = jnp.zeros((tq, B_V_DIM), jnp.float32)
    _, l1, a1, _, l2, a2 = lax.fori_loop(0, i + 1, body, (m0, l0, a0, m0, l0, a0))

    lam_p = lam_ref[...]
    e1 = jnp.exp(jnp.sum(lam_p[0:1] * lam_p[1:2], axis=-1, keepdims=True))
    e2 = jnp.exp(jnp.sum(lam_p[2:3] * lam_p[3:4], axis=-1, keepdims=True))
    lam = e1 - e2 + lam_init
    o = a1 / l1 - lam * (a2 / l2)
    ms = jnp.mean(o * o, axis=-1, keepdims=True)
    o = o * lax.rsqrt(ms + LN_EPS) * g_ref[...] * (1.0 - lam_init)
    o_ref[...] = o.astype(o_ref.dtype)


def _diff_attn(hb, slopes, lam_p, subln_g, lam_init):
    bsz, s, _ = hb.shape
    tq = min(DIFF_TQ, s)
    qb, kb, vb = HB_BQ // LANES, HB_BK // LANES, HB_BV // LANES
    kern = functools.partial(_diff_attn_kernel, lam_init=lam_init, tq=tq)
    return pl.pallas_call(
        kern,
        grid=(bsz, B_HEADS, s // tq),
        in_specs=[
            pl.BlockSpec(memory_space=pltpu.SMEM),
            pl.BlockSpec((None, tq, LANES), lambda b, h, i: (b, i, qb + h)),
            pl.BlockSpec((None, s, LANES), lambda b, h, i: (b, 0, kb + h)),
            pl.BlockSpec((None, s, LANES), lambda b, h, i: (b, 0, vb + h)),
            pl.BlockSpec(lam_p.shape, lambda b, h, i: (0, 0)),
            pl.BlockSpec(subln_g.shape, lambda b, h, i: (0, 0)),
        ],
        out_specs=pl.BlockSpec((None, tq, LANES), lambda b, h, i: (b, i, h)),
        out_shape=jax.ShapeDtypeStruct((bsz, s, B_WIDTH), MXU_DTYPE),
        compiler_params=pltpu.CompilerParams(
            dimension_semantics=("parallel", "parallel", "arbitrary"),
            vmem_limit_bytes=VMEM_LIMIT),
        name="diff_attn",
    )(slopes, hb, hb, hb, lam_p, subln_g)


def _dsa_kernel(slopes_ref, cq_ref, iq_ref, wq_ref, ck_ref, cvt_ref, ik_ref, o_ref,
                key_ref, qt_ref, qbd_ref, m_ref, l_ref, acc_ref, *, topk, tq, grp):
    i = pl.program_id(1)
    n_keys = (i + 1) * tq
    n_grp = (n_keys + grp - 1) // grp
    f32 = jnp.float32

    iq_t = iq_ref[...].astype(f32).T
    zpad = jnp.zeros((LANES - IDX_DIM, tq), f32)
    for hh in range(IDX_HEADS):
        blk = jnp.concatenate([iq_t[hh * IDX_DIM:(hh + 1) * IDX_DIM], zpad], axis=0)
        qt_ref[:, hh * tq:(hh + 1) * tq] = blk.astype(qt_ref.dtype)
    w_t = wq_ref[...].astype(f32).T[IKW_W_OFF:IKW_W_OFF + IDX_HEADS]
    cq_t = cq_ref[...].astype(f32).T
    row = lax.broadcasted_iota(jnp.int32, cq_t.shape, 0)
    for hh in range(C_HEADS):
        sel = (row >= hh * C_HEAD_DIM) & (row < (hh + 1) * C_HEAD_DIM)
        qbd_ref[:, hh * tq:(hh + 1) * tq] = jnp.where(sel, cq_t, 0.0).astype(qbd_ref.dtype)

    kloc = lax.broadcasted_iota(jnp.int32, (grp, tq), 0)
    qpos = i * tq + lax.broadcasted_iota(jnp.int32, (grp, tq), 1)
    qchunk = qpos // CHUNK

    def score_body(g, carry):
        start = pl.multiple_of(g * grp, grp)
        x = _dot(ik_ref[pl.ds(start, grp), :], qt_ref[...])
        sc = jnp.zeros((grp, tq), f32)
        for hh in range(IDX_HEADS):
            sc = sc + w_t[hh:hh + 1] * jnp.maximum(x[:, hh * tq:(hh + 1) * tq], 0.0)
        sc = jnp.where(sc == 0.0, 0.0, sc)
        kpos = g * grp + kloc
        sc = jnp.where((kpos // CHUNK) <= qchunk, sc, -jnp.inf)
        bits = pltpu.bitcast(sc, jnp.int32)
        key_ref[pl.ds(start, grp), :] = bits ^ ((bits >> 31) & 0x7FFFFFFF)
        return carry

    lax.fori_loop(0, n_grp, score_body, 0)

    def count(pred_fn):
        def cbody(g, acc):
            start = pl.multiple_of(g * grp, grp)
            kpos = g * grp + kloc
            hit = pred_fn(key_ref[pl.ds(start, grp), :], kpos)
            return acc + jnp.sum(jnp.where(hit, 1, 0).reshape(grp // 8, 8, tq), axis=0)
        acc = lax.fori_loop(0, n_grp, cbody, jnp.zeros((8, tq), jnp.int32))
        return jnp.sum(acc, axis=0, keepdims=True)

    def bit_body(b, u):
        cand_u = u | lax.shift_left(jnp.int32(1), 31 - b)
        cand = cand_u ^ INT_MIN
        cnt = count(lambda keys, kpos: keys >= cand)
        return jnp.where(cnt >= topk, cand_u, u)

    u = lax.fori_loop(0, 32, bit_body, jnp.zeros((1, tq), jnp.int32))
    thr = u ^ INT_MIN
    cnt_gt = count(lambda keys, kpos: keys > thr)
    cnt_ge = count(lambda keys, kpos: keys >= thr)
    room = topk - cnt_gt

    def tie_search():
        nbits = max(1, (key_ref.shape[0]).bit_length())

        def tbody(b, c):
            cand = c | lax.shift_left(jnp.int32(1), nbits - 1 - b)
            cnt = count(lambda keys, kpos: (keys == thr) & (kpos < cand))
            return jnp.where(cnt <= room, cand, c)

        return lax.fori_loop(0, nbits, tbody, jnp.zeros((1, tq), jnp.int32))

    need_tie = jnp.max(cnt_ge) > topk
    cut = lax.cond(need_tie, tie_search,
                   lambda: jnp.full((1, tq), 2 ** 30, jnp.int32))

    m_ref[...] = jnp.full(m_ref.shape, NEG_BIG, f32)
    l_ref[...] = jnp.zeros(l_ref.shape, f32)
    acc_ref[...] = jnp.zeros(acc_ref.shape, f32)

    def attn_body(g, carry):
        start = pl.multiple_of(g * grp, grp)
        kpos = g * grp + kloc
        keys = key_ref[pl.ds(start, grp), :]
        sel = (keys > thr) | ((keys == thr) & (kpos < cut))
        sel = sel & ((kpos // CHUNK) <= qchunk)
        dist = jnp.abs(qpos - kpos).astype(f32)
        s_all = _dot(ck_ref[pl.ds(start, grp), :], qbd_ref[...])
        vt = cvt_ref[g]
        for hh in range(C_HEADS):
            s = s_all[:, hh * tq:(hh + 1) * tq] - slopes_ref[hh] * dist
            s = jnp.where(sel, s, NEG_BIG)
            m_old = m_ref[hh:hh + 1, :]
            m_new = jnp.maximum(m_old, jnp.max(s, axis=0, keepdims=True))
            p = jnp.exp(s - m_new)
            alpha = jnp.exp(m_old - m_new)
            m_ref[hh:hh + 1, :] = m_new
            l_ref[hh:hh + 1, :] = alpha * l_ref[hh:hh + 1, :] + jnp.sum(p, axis=0, keepdims=True)
            rows = slice(hh * C_HEAD_DIM, (hh + 1) * C_HEAD_DIM)
            acc_ref[rows, :] = alpha * acc_ref[rows, :] + _dot(vt[rows, :], p.astype(MXU_DTYPE))
        return carry

    lax.fori_loop(0, n_grp, attn_body, 0)

    outs = []
    for hh in range(C_HEADS):
        rows = slice(hh * C_HEAD_DIM, (hh + 1) * C_HEAD_DIM)
        outs.append(acc_ref[rows, :] / l_ref[hh:hh + 1, :])
    o_ref[...] = jnp.concatenate(outs, axis=0).T.astype(o_ref.dtype)


def _dsa_attn(hb, cvt, slopes):
    bsz, s, _ = hb.shape
    tq = min(DSA_TQ, s)
    grp = min(DSA_G, s)
    topk = min(TOPK_MAX, s // 4)
    cqb, ckb = HB_CQ // C_WIDTH, HB_CK // C_WIDTH
    iqb, ikwb = HB_IQ // C_WIDTH, HB_IKW // LANES
    kern = functools.partial(_dsa_kernel, topk=topk, tq=tq, grp=grp)
    return pl.pallas_call(
        kern,
        grid=(bsz, s // tq),
        in_specs=[
            pl.BlockSpec(memory_space=pltpu.SMEM),
            pl.BlockSpec((None, tq, C_WIDTH), lambda b, i: (b, i, cqb)),
            pl.BlockSpec((None, tq, C_WIDTH), lambda b, i: (b, i, iqb)),
            pl.BlockSpec((None, tq, LANES), lambda b, i: (b, i, ikwb)),
            pl.BlockSpec((None, s, C_WIDTH), lambda b, i: (b, 0, ckb)),
            pl.BlockSpec((None, s // grp, C_WIDTH, grp), lambda b, i: (b, 0, 0, 0)),
            pl.BlockSpec((None, s, LANES), lambda b, i: (b, 0, ikwb)),
        ],
        out_specs=pl.BlockSpec((None, tq, C_WIDTH), lambda b, i: (b, i, 0)),
        out_shape=jax.ShapeDtypeStruct((bsz, s, C_WIDTH), MXU_DTYPE),
        scratch_shapes=[
            pltpu.VMEM((s, tq), jnp.int32),
            pltpu.VMEM((LANES, IDX_HEADS * tq), MXU_DTYPE),
            pltpu.VMEM((C_WIDTH, C_HEADS * tq), MXU_DTYPE),
            pltpu.VMEM((8, tq), jnp.float32),
            pltpu.VMEM((8, tq), jnp.float32),
            pltpu.VMEM((C_WIDTH, tq), jnp.float32),
        ],
        compiler_params=pltpu.CompilerParams(
            dimension_semantics=("parallel", "arbitrary"), vmem_limit_bytes=VMEM_LIMIT),
        name="dsa_attn",
    )(slopes, hb, hb, hb, hb, cvt, hb)


def _outproj_kernel(oa_ref, ob_ref, oc_ref, x_ref, w_ref, g_ref, b_ref, o_ref, *, alpha):
    y = _dot(oa_ref[...], w_ref[0:A_WIDTH, :])
    y = y + _dot(ob_ref[...], w_ref[A_WIDTH:A_WIDTH + B_WIDTH, :])
    y = y + _dot(oc_ref[...], w_ref[A_WIDTH + B_WIDTH:, :])
    z = alpha * x_ref[...] + y
    o_ref[...] = _layer_norm_rows(z, g_ref[...], b_ref[...])


def _out_proj(oa, ob, oc, x, w, g, b, alpha):
    bsz, s, d = x.shape
    tm = min(ROW_TILE, s)
    row = lambda width: pl.BlockSpec((None, tm, width), lambda bb, i: (bb, i, 0))
    full = lambda a: pl.BlockSpec(a.shape, lambda bb, i: (0, 0))
    return pl.pallas_call(
        functools.partial(_outproj_kernel, alpha=alpha),
        grid=(bsz, s // tm),
        in_specs=[row(A_WIDTH), row(B_WIDTH), row(C_WIDTH), row(d), full(w), full(g), full(b)],
        out_specs=row(d),
        out_shape=jax.ShapeDtypeStruct((bsz, s, d), jnp.float32),
        compiler_params=pltpu.CompilerParams(
            dimension_semantics=("parallel", "parallel"), vmem_limit_bytes=VMEM_LIMIT),
        name="out_proj",
    )(oa, ob, oc, x, w, g, b)


def _ffn_kernel(x_ref, wg_ref, wu_ref, wd_ref, g_ref, b_ref, o_ref, acc_ref, *, alpha, fc):
    x = x_ref[...]
    xb = x.astype(MXU_DTYPE)
    hidden = wg_ref.shape[1]
    for c0 in range(0, hidden, fc):
        gate = _dot(xb, wg_ref[:, c0:c0 + fc])
        up = _dot(xb, wu_ref[:, c0:c0 + fc])
        hid = (jax.nn.silu(gate) * up).astype(MXU_DTYPE)
        part = _dot(hid, wd_ref[c0:c0 + fc, :])
        if c0 == 0:
            acc_ref[...] = part
        else:
            acc_ref[...] += part
    z = alpha * x + acc_ref[...]
    o_ref[...] = _layer_norm_rows(z, g_ref[...], b_ref[...])


def _ffn(x, wg, wu, wd, g, b, alpha):
    bsz, s, d = x.shape
    tm = min(ROW_TILE, s)
    row = pl.BlockSpec((None, tm, d), lambda bb, i: (bb, i, 0))
    full = lambda a: pl.BlockSpec(a.shape, lambda bb, i: (0, 0))
    return pl.pallas_call(
        functools.partial(_ffn_kernel, alpha=alpha, fc=256),
        grid=(bsz, s // tm),
        in_specs=[row, full(wg), full(wu), full(wd), full(g), full(b)],
        out_specs=row,
        out_shape=jax.ShapeDtypeStruct((bsz, s, d), jnp.float32),
        scratch_shapes=[pltpu.VMEM((tm, d), jnp.float32)],
        compiler_params=pltpu.CompilerParams(
            dimension_semantics=("parallel", "parallel"), vmem_limit_bytes=VMEM_LIMIT),
        name="ffn",
    )(x, wg, wu, wd, g, b)


def _alibi_slopes(n):
    return jnp.asarray([2.0 ** (-8.0 * (h + 1) / n) for h in range(n)], dtype=jnp.float32)


def _prep_in_weights(w):
    off_bq = 2 * A_WIDTH
    off_cq = off_bq + 3 * B_WIDTH
    off_iq = off_cq + 3 * C_WIDTH
    off_ik = off_iq + IDX_HEADS * IDX_DIM
    off_iw = off_ik + IDX_DIM
    q_scale = B_QK_DIM ** -0.5
    c_scale = C_HEAD_DIM ** -0.5
    i_scale = (IDX_HEADS ** -0.5) * (IDX_DIM ** -0.5)
    d = w.shape[0]
    wa = w[:, :off_bq]
    wb = jnp.concatenate([
        w[:, off_bq:off_bq + B_WIDTH] * q_scale,
        w[:, off_bq + B_WIDTH:off_cq],
        w[:, off_cq:off_cq + C_WIDTH] * c_scale,
        w[:, off_cq + C_WIDTH:off_iw],
        w[:, off_iw:off_iw + IDX_HEADS] * i_scale,
        jnp.zeros((d, LANES - IDX_DIM - IDX_HEADS), w.dtype),
    ], axis=1)
    return wa.astype(MXU_DTYPE), wb.astype(MXU_DTYPE)


def kernel(x, w_in, gmlp_w_s, gmlp_b_s, gmlp_ln_g, gmlp_ln_b, lam_q1, lam_k1, lam_q2, lam_k2,
           diff_subln_g, w_out, ln1_g, ln1_b, w_gu, w_down, ln2_g, ln2_b):
    depth = w_in.shape[0]
    alpha = (2 * depth) ** 0.25
    hidden = w_down.shape[1]
    slopes_b = _alibi_slopes(B_HEADS)
    slopes_c = _alibi_slopes(C_HEADS)
    for l in range(depth):
        lam_init = 0.8 - 0.6 * math.exp(-0.3 * l)
        wa, wb = _prep_in_weights(w_in[l])
        ha, hb, cvt = _in_proj(x, wa, wb)
        out_a = _gmlp(ha, gmlp_w_s[l], gmlp_b_s[l].T,
                      gmlp_ln_g[l].reshape(1, A_WIDTH), gmlp_ln_b[l].reshape(1, A_WIDTH))
        lam_p = jnp.stack([lam_q1[l], lam_k1[l], lam_q2[l], lam_k2[l]]).astype(jnp.float32)
        out_b = _diff_attn(hb, slopes_b, lam_p, diff_subln_g[l].reshape(1, B_V_DIM), lam_init)
        out_c = _dsa_attn(hb, cvt, slopes_c)
        x = _out_proj(out_a, out_b, out_c, x, w_out[l].astype(MXU_DTYPE),
                      ln1_g[l].reshape(1, -1), ln1_b[l].reshape(1, -1), alpha)
        x = _ffn(x, w_gu[l, :, :hidden].astype(MXU_DTYPE), w_gu[l, :, hidden:].astype(MXU_DTYPE),
                 w_down[l].astype(MXU_DTYPE), ln2_g[l].reshape(1, -1), ln2_b[l].reshape(1, -1), alpha)
    return x
```

```python
import functools
import math
import struct

import jax
import jax.numpy as jnp
from jax import lax
from jax.experimental import pallas as pl
from jax.experimental.pallas import tpu as pltpu

CHUNK = 64
GMLP_BLOCK = 128
A_GROUPS, A_GDIM = 4, 64
A_WIDTH = A_GROUPS * A_GDIM
B_HEADS, B_QK_DIM = 4, 64
B_V_DIM = 2 * B_QK_DIM
B_WIDTH = B_HEADS * B_V_DIM
C_HEADS, C_HEAD_DIM = 4, 64
C_WIDTH = C_HEADS * C_HEAD_DIM
IDX_HEADS, IDX_DIM = 8, 32
TOPK_MAX = 256
LN_EPS = 1e-5
LOG2E = 1.4426950408889634

LANES = 128
SUBLANES = 8
WORD_BITS = 32
KEY_BLOCK = 256

HB_BQ = 0
HB_BK = HB_BQ + B_WIDTH
HB_CK = HB_BK + 2 * B_WIDTH
HB_CQ = HB_CK + 2 * C_WIDTH
HB_IQ = HB_CQ + C_WIDTH
HB_IKW = HB_IQ + IDX_HEADS * IDX_DIM
HB_WIDTH = HB_IKW + LANES
IKW_W_OFF = IDX_DIM
B_FEAT_LANE = B_QK_DIM
C_FEAT_LANE = 0

MXU_DTYPE = jnp.bfloat16
NEG_BIG = -1e30
INT_MIN = -(2 ** 31)

ROW_TILE = 512
DSA_TQ = 128
VMEM_LIMIT = 56 * 1024 * 1024


def _dot(a, b):
    return jnp.dot(a, b, preferred_element_type=jnp.float32)


def _layer_norm_rows(z, g, b):
    mu = jnp.mean(z, axis=-1, keepdims=True)
    zc = z - mu
    var = jnp.mean(zc * zc, axis=-1, keepdims=True)
    return zc * lax.rsqrt(var + LN_EPS) * g + b


def _bf16_round(v):
    bits = struct.unpack("<I", struct.pack("<f", v))[0]
    bits = ((bits + 0x7FFF + ((bits >> 16) & 1)) >> 16) << 16
    return struct.unpack("<f", struct.pack("<I", bits & 0xFFFFFFFF))[0]


def _alibi_coefs(n):
    out = []
    for h in range(n):
        c = (2.0 ** (-8.0 * (h + 1) / n)) * LOG2E
        hi = _bf16_round(c)
        out.append((hi, c - hi, c))
    return out


def _inproj_kernel(x_ref, wa_ref, wb_ref, wv_ref, ha_ref, hb_ref, bvt_ref, cvt_ref):
    tm = x_ref.shape[0]
    xb = x_ref[...].astype(MXU_DTYPE)
    ha_ref[...] = _dot(xb, wa_ref[...])

    kloc = (lax.broadcasted_iota(jnp.int32, (tm, LANES), 0) % KEY_BLOCK).astype(jnp.float32)
    lane = lax.broadcasted_iota(jnp.int32, (tm, LANES), 1)
    zero = jnp.zeros((tm, LANES), jnp.float32)
    feat_b = jnp.where((lane == B_FEAT_LANE) | (lane == B_FEAT_LANE + 1), kloc, 0.0)
    feat_c = jnp.where((lane >= C_FEAT_LANE) & (lane < C_FEAT_LANE + 4), kloc, 0.0)
    add_b = jnp.concatenate([feat_b] * 4, axis=1)
    add_c = jnp.concatenate([zero, feat_c, zero, feat_c], axis=1)
    segments = [(HB_BQ, HB_BK, None), (HB_BK, HB_BK + 512, add_b), (HB_BK + 512, HB_CK, add_b),
                (HB_CK, HB_CQ, add_c), (HB_CQ, HB_WIDTH, None)]
    for c0, c1, add in segments:
        y = _dot(xb, wb_ref[:, c0:c1])
        if add is not None:
            y = y + add
        hb_ref[:, c0:c1] = y.astype(hb_ref.dtype)

    yv = _dot(xb, wv_ref[...])
    for g in range(tm // KEY_BLOCK):
        rows = slice(g * KEY_BLOCK, (g + 1) * KEY_BLOCK)
        bvt_ref[g] = yv[rows, 0:B_WIDTH].T.astype(bvt_ref.dtype)
        cvt_ref[g] = yv[rows, B_WIDTH:].T.astype(cvt_ref.dtype)


def _in_proj(x, wa, wb, wv):
    bsz, s, d = x.shape
    tm = min(ROW_TILE, s)
    gpt = tm // KEY_BLOCK
    full = lambda a: pl.BlockSpec(a.shape, lambda b, i: (0, 0))
    return pl.pallas_call(
        _inproj_kernel,
        grid=(bsz, s // tm),
        in_specs=[pl.BlockSpec((None, tm, d), lambda b, i: (b, i, 0)), full(wa), full(wb), full(wv)],
        out_specs=[
            pl.BlockSpec((None, tm, 2 * A_WIDTH), lambda b, i: (b, i, 0)),
            pl.BlockSpec((None, tm, HB_WIDTH), lambda b, i: (b, i, 0)),
            pl.BlockSpec((None, gpt, B_WIDTH, KEY_BLOCK), lambda b, i: (b, i, 0, 0)),
            pl.BlockSpec((None, gpt, C_WIDTH, KEY_BLOCK), lambda b, i: (b, i, 0, 0)),
        ],
        out_shape=[
            jax.ShapeDtypeStruct((bsz, s, 2 * A_WIDTH), jnp.float32),
            jax.ShapeDtypeStruct((bsz, s, HB_WIDTH), MXU_DTYPE),
            jax.ShapeDtypeStruct((bsz, s // KEY_BLOCK, B_WIDTH, KEY_BLOCK), MXU_DTYPE),
            jax.ShapeDtypeStruct((bsz, s // KEY_BLOCK, C_WIDTH, KEY_BLOCK), MXU_DTYPE),
        ],
        compiler_params=pltpu.CompilerParams(
            dimension_semantics=("parallel", "parallel"), vmem_limit_bytes=VMEM_LIMIT),
        name="in_proj",
    )(x, wa, wb, wv)


def _gmlp_kernel(ha_ref, ws_ref, bs_ref, g_ref, b_ref, o_ref):
    nblk = ha_ref.shape[0] // GMLP_BLOCK
    r = lax.broadcasted_iota(jnp.int32, (GMLP_BLOCK, GMLP_BLOCK), 0) // CHUNK
    c = lax.broadcasted_iota(jnp.int32, (GMLP_BLOCK, GMLP_BLOCK), 1) // CHUNK
    causal = c <= r
    for blk in range(nblk):
        rows = slice(blk * GMLP_BLOCK, (blk + 1) * GMLP_BLOCK)
        u = jax.nn.gelu(ha_ref[rows, 0:A_WIDTH])
        v = jax.nn.gelu(ha_ref[rows, A_WIDTH:2 * A_WIDTH])
        outs = []
        for g in range(A_GROUPS):
            cols = slice(g * A_GDIM, (g + 1) * A_GDIM)
            vg = _layer_norm_rows(v[:, cols], g_ref[:, cols], b_ref[:, cols])
            w = jnp.where(causal, ws_ref[g], 0.0).astype(MXU_DTYPE)
            sg = _dot(w, vg.astype(MXU_DTYPE)) + bs_ref[:, g:g + 1]
            outs.append(u[:, cols] * sg)
        o_ref[rows, :] = jnp.concatenate(outs, axis=-1).astype(o_ref.dtype)


def _gmlp(ha, w_s, b_s_t, ln_g, ln_b):
    bsz, s, _ = ha.shape
    tm = min(ROW_TILE, s)
    return pl.pallas_call(
        _gmlp_kernel,
        grid=(bsz, s // tm),
        in_specs=[
            pl.BlockSpec((None, tm, 2 * A_WIDTH), lambda b, i: (b, i, 0)),
            pl.BlockSpec(w_s.shape, lambda b, i: (0, 0, 0)),
            pl.BlockSpec(b_s_t.shape, lambda b, i: (0, 0)),
            pl.BlockSpec(ln_g.shape, lambda b, i: (0, 0)),
            pl.BlockSpec(ln_b.shape, lambda b, i: (0, 0)),
        ],
        out_specs=pl.BlockSpec((None, tm, A_WIDTH), lambda b, i: (b, i, 0)),
        out_shape=jax.ShapeDtypeStruct((bsz, s, A_WIDTH), MXU_DTYPE),
        compiler_params=pltpu.CompilerParams(
            dimension_semantics=("parallel", "parallel"), vmem_limit_bytes=VMEM_LIMIT),
        name="gmlp",
    )(ha, w_s, b_s_t, ln_g, ln_b)


def _softmax_step(s, coff, m, l):
    m_new = jnp.maximum(m, jnp.max(s, axis=0, keepdims=True) + coff)
    p = jnp.exp2(s - (m_new - coff))
    alpha = jnp.exp2(m - m_new)
    l_new = alpha * l + jnp.sum(p, axis=0, keepdims=True)
    return m_new, l_new, alpha, p


def _diff_attn_kernel(coef_ref, q_ref, k1_ref, k2_ref, vt_ref, lam_ref, g_ref, o_ref,
                      w1_ref, w2_ref, acc_ref, sa_ref, sb_ref, *, lam_init):
    f32 = jnp.float32
    kb = KEY_BLOCK
    h = pl.program_id(1)
    i = pl.program_id(2)
    c_hi, c_lo, c = coef_ref[4 * h], coef_ref[4 * h + 1], coef_ref[4 * h + 2]

    q_t = q_ref[...].astype(f32).T
    frow = lax.broadcasted_iota(jnp.int32, (LANES - B_QK_DIM, kb), 0)
    feat = jnp.where(frow == 0, c_hi, jnp.where(frow == 1, c_lo, 0.0))
    w1_ref[...] = jnp.concatenate([q_t[0:B_QK_DIM], feat], axis=0).astype(w1_ref.dtype)
    w2_ref[...] = jnp.concatenate([q_t[B_QK_DIM:], feat], axis=0).astype(w2_ref.dtype)
    acc_ref[...] = jnp.zeros(acc_ref.shape, f32)
    qrow = (i * kb + lax.broadcasted_iota(jnp.int32, (1, kb), 1)).astype(f32)

    def issue_logits(j, dst_ref):
        start = pl.multiple_of(j * kb, kb)
        dst_ref[0] = _dot(k1_ref[pl.ds(start, kb), :], w1_ref[...])
        dst_ref[1] = _dot(k2_ref[pl.ds(start, kb), :], w2_ref[...])

    def consume(j, src_ref, fix, coff, state):
        m1, l1, m2, l2 = state
        s1, s2 = src_ref[0], src_ref[1]
        vt = vt_ref[j]
        if fix is not None:
            s1, s2 = s1 + fix, s2 + fix
        m1, l1, a1, p1 = _softmax_step(s1, coff, m1, l1)
        m2, l2, a2, p2 = _softmax_step(s2, coff, m2, l2)
        acc_ref[0] = a1 * acc_ref[0] + _dot(vt, p1.astype(MXU_DTYPE))
        acc_ref[1] = a2 * acc_ref[1] + _dot(vt, p2.astype(MXU_DTYPE))
        return m1, l1, m2, l2

    def past_block(j, cur_ref, nxt_ref, state):
        issue_logits(j + 1, nxt_ref)
        coff = c * ((j * kb).astype(f32) - qrow)
        return consume(j, cur_ref, None, coff, state)

    def diag_block(cur_ref, state):
        kloc = lax.broadcasted_iota(jnp.int32, (kb, kb), 0)
        qloc = lax.broadcasted_iota(jnp.int32, (kb, kb), 1)
        fix = -c * (jnp.abs(qloc - kloc) + kloc).astype(f32)
        fix = jnp.where((kloc // CHUNK) <= (qloc // CHUNK), fix, NEG_BIG)
        return consume(i, cur_ref, fix, jnp.zeros((1, kb), f32), state)

    def pair(t, state):
        state = past_block(2 * t, sa_ref, sb_ref, state)
        return past_block(2 * t + 1, sb_ref, sa_ref, state)

    def odd_tail(state):
        return diag_block(sb_ref, past_block(i - 1, sa_ref, sb_ref, state))

    m0 = jnp.full((1, kb), NEG_BIG, f32)
    l0 = jnp.zeros((1, kb), f32)
    issue_logits(0, sa_ref)
    state = lax.fori_loop(0, i // 2, pair, (m0, l0, m0, l0))
    _, l1, _, l2 = lax.cond(i % 2 == 1, odd_tail, lambda st: diag_block(sa_ref, st), state)

    lam_p = lam_ref[...]
    e1 = jnp.exp(jnp.sum(lam_p[0:1] * lam_p[1:2], axis=-1, keepdims=True))
    e2 = jnp.exp(jnp.sum(lam_p[2:3] * lam_p[3:4], axis=-1, keepdims=True))
    lam = e1 - e2 + lam_init
    o_t = acc_ref[0] / l1 - lam * (acc_ref[1] / l2)
    ms = jnp.mean(o_t * o_t, axis=0, keepdims=True)
    o = (o_t * lax.rsqrt(ms + LN_EPS)).T * g_ref[...] * (1.0 - lam_init)
    o_ref[...] = o.astype(o_ref.dtype)


def _diff_attn(hb, bvt, coefs, lam_p, subln_g, lam_init):
    bsz, s, _ = hb.shape
    kb = KEY_BLOCK
    qb, kcol = HB_BQ // LANES, HB_BK // LANES
    kern = functools.partial(_diff_attn_kernel, lam_init=lam_init)
    return pl.pallas_call(
        kern,
        grid=(bsz, B_HEADS, s // kb),
        in_specs=[
            pl.BlockSpec(memory_space=pltpu.SMEM),
            pl.BlockSpec((None, kb, LANES), lambda b, h, i: (b, i, qb + h)),
            pl.BlockSpec((None, s, LANES), lambda b, h, i: (b, 0, kcol + 2 * h)),
            pl.BlockSpec((None, s, LANES), lambda b, h, i: (b, 0, kcol + 2 * h + 1)),
            pl.BlockSpec((None, s // kb, B_V_DIM, kb), lambda b, h, i: (b, 0, h, 0)),
            pl.BlockSpec(lam_p.shape, lambda b, h, i: (0, 0)),
            pl.BlockSpec(subln_g.shape, lambda b, h, i: (0, 0)),
        ],
        out_specs=pl.BlockSpec((None, kb, LANES), lambda b, h, i: (b, i, h)),
        out_shape=jax.ShapeDtypeStruct((bsz, s, B_WIDTH), MXU_DTYPE),
        scratch_shapes=[
            pltpu.VMEM((LANES, kb), MXU_DTYPE),
            pltpu.VMEM((LANES, kb), MXU_DTYPE),
            pltpu.VMEM((2, B_V_DIM, kb), jnp.float32),
            pltpu.VMEM((2, kb, kb), jnp.float32),
            pltpu.VMEM((2, kb, kb), jnp.float32),
        ],
        compiler_params=pltpu.CompilerParams(
            dimension_semantics=("parallel", "parallel", "arbitrary"),
            vmem_limit_bytes=VMEM_LIMIT),
        name="diff_attn",
    )(coefs, hb, hb, hb, bvt, lam_p, subln_g)


def _dsa_kernel(cq_ref, iq_ref, wq_ref, ck_ref, cvt_ref, ik_ref, o_ref,
                key_ref, planes_ref, qt_ref, wpr_ref, acc_ref, sa_ref, sb_ref, *, topk, tq, coefs):
    f32 = jnp.float32
    grp = KEY_BLOCK
    i = pl.program_id(1)
    last = (i * tq) // grp
    n_grp = last + 1

    w_t = wq_ref[...].astype(f32).T[IKW_W_OFF:IKW_W_OFF + IDX_HEADS]
    iq_t = iq_ref[...].astype(f32).T
    zpad = jnp.zeros((LANES - IDX_DIM, tq), f32)
    for hh in range(IDX_HEADS):
        blk = jnp.concatenate([iq_t[hh * IDX_DIM:(hh + 1) * IDX_DIM], zpad], axis=0)
        qt_ref[:, hh * tq:(hh + 1) * tq] = blk.astype(qt_ref.dtype)
    cq_t = cq_ref[...].astype(f32).T
    zhead = jnp.zeros((C_HEAD_DIM, tq), f32)
    frow = lax.broadcasted_iota(jnp.int32, (LANES, tq), 0)
    for pr in range(C_HEADS // 2):
        halves = []
        for hb in range(2):
            hh = 2 * pr + hb
            c_hi, c_lo, _ = coefs[hh]
            qh = cq_t[hh * C_HEAD_DIM:(hh + 1) * C_HEAD_DIM]
            feat = jnp.where(frow == C_FEAT_LANE + 2 * hb, c_hi,
                             jnp.where(frow == C_FEAT_LANE + 2 * hb + 1, c_lo, 0.0))
            halves.append(jnp.concatenate([qh, zhead, feat] if hb == 0 else [zhead, qh, feat], axis=0))
        wpr_ref[pr] = jnp.concatenate(halves, axis=1).astype(wpr_ref.dtype)

    kloc = lax.broadcasted_iota(jnp.int32, (grp, tq), 0)
    qpos = i * tq + lax.broadcasted_iota(jnp.int32, (grp, tq), 1)
    qrow = i * tq + lax.broadcasted_iota(jnp.int32, (1, tq), 1)
    allowed_end = (qrow // CHUNK + 1) * CHUNK

    @pl.when(i == 0)
    def _():
        planes_ref[...] = jnp.zeros(planes_ref.shape, jnp.int32)

    def score_body(g, carry):
        start = pl.multiple_of(g * grp, grp)
        x = _dot(ik_ref[pl.ds(start, grp), :], qt_ref[...])
        sc = jnp.zeros((grp, tq), f32)
        for hh in range(IDX_HEADS):
            sc = sc + w_t[hh:hh + 1] * jnp.maximum(x[:, hh * tq:(hh + 1) * tq], 0.0)
        sc = jnp.where(sc == 0.0, 0.0, sc)
        bits = pltpu.bitcast(sc, jnp.int32)
        key = bits ^ ((bits >> 31) & 0x7FFFFFFF)
        key = jnp.where(g * grp + kloc < allowed_end, key, INT_MIN)
        key_ref[pl.ds(start, grp), :] = key
        w = [key[r * SUBLANES:(r + 1) * SUBLANES, :] for r in range(WORD_BITS)]
        j, msk = 16, 0x0000FFFF
        while j:
            k = 0
            while k < WORD_BITS:
                t = (w[k] ^ (w[k + j] >> j)) & msk
                w[k] = w[k] ^ t
                w[k + j] = w[k + j] ^ (t << j)
                k = (k + j + 1) & ~j
            j >>= 1
            msk = (msk ^ (msk << j)) & 0xFFFFFFFF
        w[0] = ~w[0]
        prow = pl.multiple_of(g * SUBLANES, SUBLANES)
        for r in range(WORD_BITS):
            planes_ref[r, pl.ds(prow, SUBLANES), :] = w[r]
        return carry

    def score_pair(t, carry):
        score_body(2 * t, carry)
        return score_body(2 * t + 1, carry)

    lax.fori_loop(0, n_grp // 2, score_pair, 0)

    @pl.when(n_grp % 2 == 1)
    def _():
        score_body(n_grp - 1, 0)

    nrow = planes_ref.shape[1]
    prow_i = lax.broadcasted_iota(jnp.int32, (nrow, tq), 0)
    pbase = (prow_i >> 3) * grp + (prow_i & 7)

    def top_bits(n):
        n = jnp.clip(n, 0, WORD_BITS)
        return jnp.where(n > 0, lax.shift_left(jnp.int32(-1), (WORD_BITS - n) & (WORD_BITS - 1)), 0)

    def colsum(words):
        c = lax.population_count(words).reshape(nrow // SUBLANES, SUBLANES, tq)
        return jnp.sum(jnp.sum(c, axis=0), axis=0, keepdims=True)

    def bit_body(p, carry):
        cand, above, u = carry
        ones = cand & planes_ref[p]
        c1 = colsum(ones)
        take = (above + c1) >= topk
        cand = jnp.where(take, ones, cand ^ ones)
        above = jnp.where(take, above, above + c1)
        u = u | jnp.where(take, lax.shift_left(jnp.int32(1), WORD_BITS - 1 - p), 0)
        return cand, above, u

    cand0 = top_bits((allowed_end - (prow_i >> 3) * grp) >> 3)
    zero_row = jnp.zeros((1, tq), jnp.int32)
    ties, cnt_gt, u = lax.fori_loop(0, WORD_BITS, bit_body, (cand0, zero_row, zero_row))
    thr = u ^ INT_MIN
    room = topk - cnt_gt

    def tie_search():
        nbits = max(1, (key_ref.shape[0]).bit_length())

        def tbody(b, c):
            cnd = c | lax.shift_left(jnp.int32(1), nbits - 1 - b)
            cnt = colsum(ties & top_bits((cnd - pbase + 7) >> 3))
            return jnp.where(cnt <= room, cnd, c)

        return lax.fori_loop(0, nbits, tbody, zero_row)

    need_tie = jnp.max(colsum(ties) - room) > 0
    cut = lax.cond(need_tie, tie_search, lambda: jnp.full((1, tq), 2 ** 30, jnp.int32))
    cut = jnp.minimum(cut, allowed_end)

    acc_ref[...] = jnp.zeros(acc_ref.shape, f32)
    qrow_f = qrow.astype(f32)

    def issue_logits(g, dst_ref):
        start = pl.multiple_of(g * grp, grp)
        for pr in range(C_HEADS // 2):
            dst_ref[pr] = _dot(ck_ref[pl.ds(start, grp), pr * 2 * LANES:(pr + 1) * 2 * LANES], wpr_ref[pr])

    def consume(g, src_ref, exact_bias, state):
        start = pl.multiple_of(g * grp, grp)
        kpos = g * grp + kloc
        keys = key_ref[pl.ds(start, grp), :]
        sel = (keys > thr) | ((keys == thr) & (kpos < cut))
        vt = cvt_ref[g]
        if exact_bias:
            dist = (jnp.abs(qpos - kpos) + kloc).astype(f32)
        else:
            base = (g * grp).astype(f32) - qrow_f
        out = []
        for hh in range(C_HEADS):
            c = coefs[hh][2]
            s = src_ref[hh // 2, :, (hh % 2) * tq:(hh % 2 + 1) * tq]
            if exact_bias:
                s = s - c * dist
                coff = jnp.zeros((1, tq), f32)
            else:
                coff = c * base
            s = jnp.where(sel, s, NEG_BIG)
            m_new, l_new, alpha, p = _softmax_step(s, coff, state[2 * hh], state[2 * hh + 1])
            out += [m_new, l_new]
            rows = slice(hh * C_HEAD_DIM, (hh + 1) * C_HEAD_DIM)
            acc_ref[hh] = alpha * acc_ref[hh] + _dot(vt[rows, :], p.astype(MXU_DTYPE))
        return tuple(out)

    def past_group(g, cur_ref, nxt_ref, state):
        issue_logits(g + 1, nxt_ref)
        return consume(g, cur_ref, False, state)

    def pair(t, state):
        state = past_group(2 * t, sa_ref, sb_ref, state)
        return past_group(2 * t + 1, sb_ref, sa_ref, state)

    def odd_tail(state):
        return consume(last, sb_ref, True, past_group(last - 1, sa_ref, sb_ref, state))

    m0 = jnp.full((1, tq), NEG_BIG, f32)
    l0 = jnp.zeros((1, tq), f32)
    issue_logits(0, sa_ref)
    state = lax.fori_loop(0, last // 2, pair, (m0, l0) * C_HEADS)
    state = lax.cond(last % 2 == 1, odd_tail, lambda st: consume(last, sa_ref, True, st), state)

    outs = [acc_ref[hh] / state[2 * hh + 1] for hh in range(C_HEADS)]
    o_ref[...] = jnp.concatenate(outs, axis=0).T.astype(o_ref.dtype)


def _dsa_attn(hb, cvt, coefs):
    bsz, s, _ = hb.shape
    tq = min(DSA_TQ, s)
    grp = KEY_BLOCK
    topk = min(TOPK_MAX, s // 4)
    cqb, iqb = HB_CQ // C_WIDTH, HB_IQ // C_WIDTH
    ckb, ikwb = HB_CK // (2 * C_WIDTH), HB_IKW // LANES
    kern = functools.partial(_dsa_kernel, topk=topk, tq=tq, coefs=coefs)
    return pl.pallas_call(
        kern,
        grid=(bsz, s // tq),
        in_specs=[
            pl.BlockSpec((None, tq, C_WIDTH), lambda b, i: (b, i, cqb)),
            pl.BlockSpec((None, tq, C_WIDTH), lambda b, i: (b, i, iqb)),
            pl.BlockSpec((None, tq, LANES), lambda b, i: (b, i, ikwb)),
            pl.BlockSpec((None, s, 2 * C_WIDTH), lambda b, i: (b, 0, ckb)),
            pl.BlockSpec((None, s // grp, C_WIDTH, grp), lambda b, i: (b, 0, 0, 0)),
            pl.BlockSpec((None, s, LANES), lambda b, i: (b, 0, ikwb)),
        ],
        out_specs=pl.BlockSpec((None, tq, C_WIDTH), lambda b, i: (b, i, 0)),
        out_shape=jax.ShapeDtypeStruct((bsz, s, C_WIDTH), MXU_DTYPE),
        scratch_shapes=[
            pltpu.VMEM((s, tq), jnp.int32),
            pltpu.VMEM((WORD_BITS, (s // grp) * SUBLANES, tq), jnp.int32),
            pltpu.VMEM((LANES, IDX_HEADS * tq), MXU_DTYPE),
            pltpu.VMEM((C_HEADS // 2, 2 * LANES, 2 * tq), MXU_DTYPE),
            pltpu.VMEM((C_HEADS, C_HEAD_DIM, tq), jnp.float32),
            pltpu.VMEM((C_HEADS // 2, grp, 2 * tq), jnp.float32),
            pltpu.VMEM((C_HEADS // 2, grp, 2 * tq), jnp.float32),
        ],
        compiler_params=pltpu.CompilerParams(
            dimension_semantics=("parallel", "arbitrary"), vmem_limit_bytes=VMEM_LIMIT),
        name="dsa_attn",
    )(hb, hb, hb, hb, cvt, hb)


def _outproj_kernel(oa_ref, ob_ref, oc_ref, x_ref, w_ref, g_ref, b_ref, o_ref, *, alpha):
    y = _dot(oa_ref[...], w_ref[0:A_WIDTH, :])
    y = y + _dot(ob_ref[...], w_ref[A_WIDTH:A_WIDTH + B_WIDTH, :])
    y = y + _dot(oc_ref[...], w_ref[A_WIDTH + B_WIDTH:, :])
    z = alpha * x_ref[...] + y
    o_ref[...] = _layer_norm_rows(z, g_ref[...], b_ref[...])


def _out_proj(oa, ob, oc, x, w, g, b, alpha):
    bsz, s, d = x.shape
    tm = min(ROW_TILE, s)
    row = lambda width: pl.BlockSpec((None, tm, width), lambda bb, i: (bb, i, 0))
    full = lambda a: pl.BlockSpec(a.shape, lambda bb, i: (0, 0))
    return pl.pallas_call(
        functools.partial(_outproj_kernel, alpha=alpha),
        grid=(bsz, s // tm),
        in_specs=[row(A_WIDTH), row(B_WIDTH), row(C_WIDTH), row(d), full(w), full(g), full(b)],
        out_specs=row(d),
        out_shape=jax.ShapeDtypeStruct((bsz, s, d), jnp.float32),
        compiler_params=pltpu.CompilerParams(
            dimension_semantics=("parallel", "parallel"), vmem_limit_bytes=VMEM_LIMIT),
        name="out_proj",
    )(oa, ob, oc, x, w, g, b)


def _ffn_kernel(x_ref, wg_ref, wu_ref, wd_ref, g_ref, b_ref, o_ref, acc_ref, *, alpha, fc):
    x = x_ref[...]
    xb = x.astype(MXU_DTYPE)
    hidden = wg_ref.shape[1]
    for c0 in range(0, hidden, fc):
        gate = _dot(xb, wg_ref[:, c0:c0 + fc])
        up = _dot(xb, wu_ref[:, c0:c0 + fc])
        hid = (jax.nn.silu(gate) * up).astype(MXU_DTYPE)
        part = _dot(hid, wd_ref[c0:c0 + fc, :])
        if c0 == 0:
            acc_ref[...] = part
        else:
            acc_ref[...] += part
    z = alpha * x + acc_ref[...]
    o_ref[...] = _layer_norm_rows(z, g_ref[...], b_ref[...])


def _ffn(x, wg, wu, wd, g, b, alpha):
    bsz, s, d = x.shape
    tm = min(ROW_TILE, s)
    row = pl.BlockSpec((None, tm, d), lambda bb, i: (bb, i, 0))
    full = lambda a: pl.BlockSpec(a.shape, lambda bb, i: (0, 0))
    return pl.pallas_call(
        functools.partial(_ffn_kernel, alpha=alpha, fc=256),
        grid=(bsz, s // tm),
        in_specs=[row, full(wg), full(wu), full(wd), full(g), full(b)],
        out_specs=row,
        out_shape=jax.ShapeDtypeStruct((bsz, s, d), jnp.float32),
        scratch_shapes=[pltpu.VMEM((tm, d), jnp.float32)],
        compiler_params=pltpu.CompilerParams(
            dimension_semantics=("parallel", "parallel"), vmem_limit_bytes=VMEM_LIMIT),
        name="ffn",
    )(x, wg, wu, wd, g, b)


def _prep_in_weights(w):
    d = w.shape[0]
    off_bq = 2 * A_WIDTH
    off_bk = off_bq + B_WIDTH
    off_bv = off_bk + B_WIDTH
    off_cq = off_bv + B_WIDTH
    off_ck = off_cq + C_WIDTH
    off_cv = off_ck + C_WIDTH
    off_iq = off_cv + C_WIDTH
    off_ik = off_iq + IDX_HEADS * IDX_DIM
    off_iw = off_ik + IDX_DIM
    q_scale = (B_QK_DIM ** -0.5) * LOG2E
    c_scale = (C_HEAD_DIM ** -0.5) * LOG2E
    i_scale = (IDX_HEADS ** -0.5) * (IDX_DIM ** -0.5)
    zeros = lambda n: jnp.zeros((d, n), w.dtype)
    cols = [w[:, off_bq:off_bk] * q_scale]
    bk = w[:, off_bk:off_bv]
    for h in range(B_HEADS):
        for m in range(2):
            c0 = (2 * h + m) * B_QK_DIM
            cols += [bk[:, c0:c0 + B_QK_DIM], zeros(LANES - B_QK_DIM)]
    ck = w[:, off_ck:off_cv]
    for pr in range(C_HEADS // 2):
        cols += [ck[:, pr * LANES:(pr + 1) * LANES], zeros(LANES)]
    cols += [w[:, off_cq:off_ck] * c_scale, w[:, off_iq:off_iw],
             w[:, off_iw:off_iw + IDX_HEADS] * i_scale, zeros(LANES - IDX_DIM - IDX_HEADS)]
    wb = jnp.concatenate(cols, axis=1)
    wv = jnp.concatenate([w[:, off_bv:off_cq], w[:, off_cv:off_iq]], axis=1)
    return w[:, :off_bq].astype(MXU_DTYPE), wb.astype(MXU_DTYPE), wv.astype(MXU_DTYPE)


def kernel(x, w_in, gmlp_w_s, gmlp_b_s, gmlp_ln_g, gmlp_ln_b, lam_q1, lam_k1, lam_q2, lam_k2,
           diff_subln_g, w_out, ln1_g, ln1_b, w_gu, w_down, ln2_g, ln2_b):
    depth = w_in.shape[0]
    alpha = (2 * depth) ** 0.25
    hidden = w_down.shape[1]
    coefs_b = _alibi_coefs(B_HEADS)
    coefs_c = tuple(_alibi_coefs(C_HEADS))
    coef_tab = jnp.asarray([v for hi, lo, c in coefs_b for v in (hi, lo, c, 0.0)], jnp.float32)
    for l in range(depth):
        lam_init = 0.8 - 0.6 * math.exp(-0.3 * l)
        wa, wb, wv = _prep_in_weights(w_in[l])
        ha, hb, bvt, cvt = _in_proj(x, wa, wb, wv)
        out_a = _gmlp(ha, gmlp_w_s[l], gmlp_b_s[l].T,
                      gmlp_ln_g[l].reshape(1, A_WIDTH), gmlp_ln_b[l].reshape(1, A_WIDTH))
        lam_p = jnp.stack([lam_q1[l], lam_k1[l], lam_q2[l], lam_k2[l]]).astype(jnp.float32)
        out_b = _diff_attn(hb, bvt, coef_tab, lam_p, diff_subln_g[l].reshape(1, B_V_DIM), lam_init)
        out_c = _dsa_attn(hb, cvt, coefs_c)
        x = _out_proj(out_a, out_b, out_c, x, w_out[l].astype(MXU_DTYPE),
                      ln1_g[l].reshape(1, -1), ln1_b[l].reshape(1, -1), alpha)
        x = _ffn(x, w_gu[l, :, :hidden].astype(MXU_DTYPE), w_gu[l, :, hidden:].astype(MXU_DTYPE),
                 w_down[l].astype(MXU_DTYPE), ln2_g[l].reshape(1, -1), ln2_b[l].reshape(1, -1), alpha)
    return x
```

```python
import functools
import math
import struct

import jax
import jax.numpy as jnp
from jax import lax
from jax.experimental import pallas as pl
from jax.experimental.pallas import tpu as pltpu

CHUNK = 64
GMLP_BLOCK = 128
A_GROUPS, A_GDIM = 4, 64
A_WIDTH = A_GROUPS * A_GDIM
B_HEADS, B_QK_DIM = 4, 64
B_V_DIM = 2 * B_QK_DIM
B_WIDTH = B_HEADS * B_V_DIM
C_HEADS, C_HEAD_DIM = 4, 64
C_WIDTH = C_HEADS * C_HEAD_DIM
IDX_HEADS, IDX_DIM = 8, 32
TOPK_MAX = 256
LN_EPS = 1e-5
LOG2E = 1.4426950408889634

LANES = 128
SUBLANES = 8
WORD_BITS = 32
KEY_BLOCK = 256
V_PAD = 16

HB_BQ = 0
HB_BK = HB_BQ + B_WIDTH
HB_CK = HB_BK + 2 * B_WIDTH
HB_CQ = HB_CK + 2 * C_WIDTH
HB_IQ = HB_CQ + C_WIDTH
HB_IKW = HB_IQ + IDX_HEADS * IDX_DIM
HB_WIDTH = HB_IKW + LANES
IKW_W_OFF = IDX_DIM
B_FEAT_LANE = B_QK_DIM
C_FEAT_LANE = 0

MXU_DTYPE = jnp.bfloat16
NEG_BIG = -1e30
B_VT_ROWS = B_V_DIM + V_PAD
C_VT_ROWS = C_HEAD_DIM + V_PAD
INT_MIN = -(2 ** 31)

ROW_TILE = 512
DSA_TQ = 256
DIFF_TQ = 2 * KEY_BLOCK
VMEM_LIMIT = 56 * 1024 * 1024


def _dot(a, b):
    return jnp.dot(a, b, preferred_element_type=jnp.float32)


def _layer_norm_rows(z, g, b):
    mu = jnp.mean(z, axis=-1, keepdims=True)
    zc = z - mu
    var = jnp.mean(zc * zc, axis=-1, keepdims=True)
    return zc * lax.rsqrt(var + LN_EPS) * g + b


def _bf16_round(v):
    bits = struct.unpack("<I", struct.pack("<f", v))[0]
    bits = ((bits + 0x7FFF + ((bits >> 16) & 1)) >> 16) << 16
    return struct.unpack("<f", struct.pack("<I", bits & 0xFFFFFFFF))[0]


def _alibi_coefs(n):
    out = []
    for h in range(n):
        c = (2.0 ** (-8.0 * (h + 1) / n)) * LOG2E
        hi = _bf16_round(c)
        out.append((hi, c - hi, c))
    return out


def _inproj_kernel(x_ref, wa_ref, wb_ref, wv_ref, ha_ref, hb_ref, bvt_ref, cvt_ref):
    tm = x_ref.shape[0]
    xb = x_ref[...].astype(MXU_DTYPE)
    ha_ref[...] = _dot(xb, wa_ref[...])

    kloc = (lax.broadcasted_iota(jnp.int32, (tm, LANES), 0) % KEY_BLOCK).astype(jnp.float32)
    lane = lax.broadcasted_iota(jnp.int32, (tm, LANES), 1)
    zero = jnp.zeros((tm, LANES), jnp.float32)
    feat_b = jnp.where((lane == B_FEAT_LANE) | (lane == B_FEAT_LANE + 1), kloc, 0.0)
    feat_c = jnp.where((lane >= C_FEAT_LANE) & (lane < C_FEAT_LANE + 4), kloc, 0.0)
    add_b = jnp.concatenate([feat_b] * 4, axis=1)
    add_c = jnp.concatenate([zero, feat_c, zero, feat_c], axis=1)
    segments = [(HB_BQ, HB_BK, None), (HB_BK, HB_BK + 512, add_b), (HB_BK + 512, HB_CK, add_b),
                (HB_CK, HB_CQ, add_c), (HB_CQ, HB_WIDTH, None)]
    for c0, c1, add in segments:
        y = _dot(xb, wb_ref[:, c0:c1])
        if add is not None:
            y = y + add
        hb_ref[:, c0:c1] = y.astype(hb_ref.dtype)

    yv = _dot(xb, wv_ref[...])
    ones_blk = jnp.where(lax.broadcasted_iota(jnp.int32, (V_PAD, KEY_BLOCK), 0) == 0, 1.0, 0.0)

    def heads_t(y, dim):
        y_t = y.T
        parts = []
        for h in range(y.shape[1] // dim):
            parts += [y_t[h * dim:(h + 1) * dim], ones_blk]
        return jnp.concatenate(parts, axis=0)

    for g in range(tm // KEY_BLOCK):
        rows = slice(g * KEY_BLOCK, (g + 1) * KEY_BLOCK)
        bvt_ref[g] = heads_t(yv[rows, 0:B_WIDTH], B_V_DIM).astype(bvt_ref.dtype)
        cvt_ref[g] = heads_t(yv[rows, B_WIDTH:], C_HEAD_DIM).astype(cvt_ref.dtype)


def _in_proj(x, wa, wb, wv):
    bsz, s, d = x.shape
    tm = min(ROW_TILE, s)
    gpt = tm // KEY_BLOCK
    full = lambda a: pl.BlockSpec(a.shape, lambda b, i: (0, 0))
    return pl.pallas_call(
        _inproj_kernel,
        grid=(bsz, s // tm),
        in_specs=[pl.BlockSpec((None, tm, d), lambda b, i: (b, i, 0)), full(wa), full(wb), full(wv)],
        out_specs=[
            pl.BlockSpec((None, tm, 2 * A_WIDTH), lambda b, i: (b, i, 0)),
            pl.BlockSpec((None, tm, HB_WIDTH), lambda b, i: (b, i, 0)),
            pl.BlockSpec((None, gpt, B_HEADS * B_VT_ROWS, KEY_BLOCK), lambda b, i: (b, i, 0, 0)),
            pl.BlockSpec((None, gpt, C_HEADS * C_VT_ROWS, KEY_BLOCK), lambda b, i: (b, i, 0, 0)),
        ],
        out_shape=[
            jax.ShapeDtypeStruct((bsz, s, 2 * A_WIDTH), jnp.float32),
            jax.ShapeDtypeStruct((bsz, s, HB_WIDTH), MXU_DTYPE),
            jax.ShapeDtypeStruct((bsz, s // KEY_BLOCK, B_HEADS * B_VT_ROWS, KEY_BLOCK), MXU_DTYPE),
            jax.ShapeDtypeStruct((bsz, s // KEY_BLOCK, C_HEADS * C_VT_ROWS, KEY_BLOCK), MXU_DTYPE),
        ],
        compiler_params=pltpu.CompilerParams(
            dimension_semantics=("parallel", "parallel"), vmem_limit_bytes=VMEM_LIMIT),
        name="in_proj",
    )(x, wa, wb, wv)


def _gmlp_kernel(ha_ref, ws_ref, bias_ref, g_ref, b_ref, o_ref):
    f32 = jnp.float32
    nblk = ha_ref.shape[0] // GMLP_BLOCK
    r = lax.broadcasted_iota(jnp.int32, (GMLP_BLOCK, GMLP_BLOCK), 0) // CHUNK
    c = lax.broadcasted_iota(jnp.int32, (GMLP_BLOCK, GMLP_BLOCK), 1) // CHUNK
    w_all = jnp.concatenate([jnp.where(c <= r, ws_ref[g], 0.0) for g in range(A_GROUPS)],
                            axis=0).astype(MXU_DTYPE)
    ar = lax.broadcasted_iota(jnp.int32, (A_WIDTH, A_WIDTH), 0) // A_GDIM
    ac = lax.broadcasted_iota(jnp.int32, (A_WIDTH, A_WIDTH), 1) // A_GDIM
    avg = jnp.where(ar == ac, 1.0 / A_GDIM, 0.0).astype(MXU_DTYPE)
    lane_grp = lax.broadcasted_iota(jnp.int32, (GMLP_BLOCK, A_WIDTH), 1) // A_GDIM

    def group_mean(z):
        hi = z.astype(MXU_DTYPE)
        lo = (z - hi.astype(f32)).astype(MXU_DTYPE)
        return _dot(hi, avg) + _dot(lo, avg)

    u = jax.nn.gelu(ha_ref[:, 0:A_WIDTH])
    v = jax.nn.gelu(ha_ref[:, A_WIDTH:2 * A_WIDTH])
    vc = v - group_mean(v)
    vn = (vc * lax.rsqrt(group_mean(vc * vc) + LN_EPS) * g_ref[...] + b_ref[...]).astype(MXU_DTYPE)
    for blk in range(nblk):
        rows = slice(blk * GMLP_BLOCK, (blk + 1) * GMLP_BLOCK)
        z = _dot(w_all, vn[rows, :])
        sg = bias_ref[...]
        for g in range(A_GROUPS):
            sg = sg + jnp.where(lane_grp == g, z[g * GMLP_BLOCK:(g + 1) * GMLP_BLOCK], 0.0)
        o_ref[rows, :] = (u[rows, :] * sg).astype(o_ref.dtype)


def _gmlp(ha, w_s, b_s_t, ln_g, ln_b):
    bsz, s, _ = ha.shape
    tm = min(ROW_TILE, s)
    return pl.pallas_call(
        _gmlp_kernel,
        grid=(bsz, s // tm),
        in_specs=[
            pl.BlockSpec((None, tm, 2 * A_WIDTH), lambda b, i: (b, i, 0)),
            pl.BlockSpec(w_s.shape, lambda b, i: (0, 0, 0)),
            pl.BlockSpec(b_s_t.shape, lambda b, i: (0, 0)),
            pl.BlockSpec(ln_g.shape, lambda b, i: (0, 0)),
            pl.BlockSpec(ln_b.shape, lambda b, i: (0, 0)),
        ],
        out_specs=pl.BlockSpec((None, tm, A_WIDTH), lambda b, i: (b, i, 0)),
        out_shape=jax.ShapeDtypeStruct((bsz, s, A_WIDTH), MXU_DTYPE),
        compiler_params=pltpu.CompilerParams(
            dimension_semantics=("parallel", "parallel"), vmem_limit_bytes=VMEM_LIMIT),
        name="gmlp",
    )(ha, w_s, b_s_t, ln_g, ln_b)


def _softmax_step(s, coff, m):
    m_new = jnp.maximum(m, jnp.max(s, axis=0, keepdims=True) + coff)
    p = jnp.exp2(s - (m_new - coff)).astype(MXU_DTYPE)
    return m_new, jnp.exp2(m - m_new), p


def _diff_attn_kernel(coef_ref, q_ref, k1_ref, k2_ref, vt_ref, lam_ref, g_ref, o_ref,
                      w1_ref, w2_ref, acc_ref, sa_ref, sb_ref, pa_ref, pb_ref, *, lam_init):
    f32 = jnp.float32
    kb = KEY_BLOCK
    tq = DIFF_TQ
    h = pl.program_id(1)
    i = pl.program_id(2)
    c_hi, c_lo, c = coef_ref[4 * h], coef_ref[4 * h + 1], coef_ref[4 * h + 2]

    q_t = q_ref[...].astype(f32).T
    frow = lax.broadcasted_iota(jnp.int32, (LANES - B_QK_DIM, tq), 0)
    feat = jnp.where(frow == 0, c_hi, jnp.where(frow == 1, c_lo, 0.0))
    w1_ref[...] = jnp.concatenate([q_t[0:B_QK_DIM], feat], axis=0).astype(w1_ref.dtype)
    w2_ref[...] = jnp.concatenate([q_t[B_QK_DIM:], feat], axis=0).astype(w2_ref.dtype)
    acc_ref[...] = jnp.zeros(acc_ref.shape, f32)
    pb_ref[...] = jnp.zeros(pb_ref.shape, pb_ref.dtype)
    qrow = (i * tq + lax.broadcasted_iota(jnp.int32, (1, tq), 1)).astype(f32)

    def issue_logits(j, dst_ref):
        start = pl.multiple_of(j * kb, kb)
        dst_ref[0] = _dot(k1_ref[pl.ds(start, kb), :], w1_ref[...])
        dst_ref[1] = _dot(k2_ref[pl.ds(start, kb), :], w2_ref[...])

    def issue_pv(j, p_ref):
        vt = vt_ref[j]
        return _dot(vt, p_ref[0]), _dot(vt, p_ref[1])

    def step(j, s_cur, s_nxt, p_cur, p_prev, fix, coff, state):
        m1, m2, a1_prev, a2_prev = state
        s1, s2 = s_cur[0], s_cur[1]
        if fix is not None:
            s1, s2 = s1 + fix, s2 + fix
        m1, a1, p1 = _softmax_step(s1, coff, m1)
        p_cur[0] = p1
        pv1, pv2 = issue_pv(jnp.maximum(j - 1, 0), p_prev)
        m2, a2, p2 = _softmax_step(s2, coff, m2)
        p_cur[1] = p2
        if s_nxt is not None:
            issue_logits(j + 1, s_nxt)
        acc_ref[0] = a1_prev * acc_ref[0] + pv1
        acc_ref[1] = a2_prev * acc_ref[1] + pv2
        return m1, m2, a1, a2

    def past_block(j, s_cur, s_nxt, p_cur, p_prev, state):
        coff = c * ((j * kb).astype(f32) - qrow)
        return step(j, s_cur, s_nxt, p_cur, p_prev, None, coff, state)

    def pair(t, state):
        state = past_block(2 * t, sa_ref, sb_ref, pa_ref, pb_ref, state)
        return past_block(2 * t + 1, sb_ref, sa_ref, pb_ref, pa_ref, state)

    m0 = jnp.full((1, tq), NEG_BIG, f32)
    one = jnp.ones((1, tq), f32)
    issue_logits(0, sa_ref)
    state = lax.fori_loop(0, i, pair, (m0, m0, one, one))

    kloc = lax.broadcasted_iota(jnp.int32, (kb, kb), 0)
    qloc = lax.broadcasted_iota(jnp.int32, (kb, kb), 1)
    diag = -c * (jnp.abs(qloc - kloc) + kloc).astype(f32)
    diag = jnp.where((kloc // CHUNK) <= (qloc // CHUNK), diag, NEG_BIG)
    zero_row = jnp.zeros((1, kb), f32)
    lo = 2 * i
    fix = jnp.concatenate([diag, jnp.zeros((kb, kb), f32)], axis=1)
    coff = jnp.concatenate([zero_row, c * ((lo * kb).astype(f32) - qrow[:, kb:])], axis=1)
    state = step(lo, sa_ref, sb_ref, pa_ref, pb_ref, fix, coff, state)
    fix = jnp.concatenate([jnp.full((kb, kb), NEG_BIG, f32), diag], axis=1)
    _, _, a1, a2 = step(lo + 1, sb_ref, None, pb_ref, pa_ref, fix, jnp.zeros((1, tq), f32), state)
    pv1, pv2 = issue_pv(lo + 1, pb_ref)
    n1 = a1 * acc_ref[0] + pv1
    n2 = a2 * acc_ref[1] + pv2
    o1 = n1[0:B_V_DIM] / n1[B_V_DIM:B_V_DIM + 1]
    o2 = n2[0:B_V_DIM] / n2[B_V_DIM:B_V_DIM + 1]

    lam_p = lam_ref[...]
    e1 = jnp.exp(jnp.sum(lam_p[0:1] * lam_p[1:2], axis=-1, keepdims=True))
    e2 = jnp.exp(jnp.sum(lam_p[2:3] * lam_p[3:4], axis=-1, keepdims=True))
    lam = e1 - e2 + lam_init
    o_t = o1 - lam * o2
    ms = jnp.mean(o_t * o_t, axis=0, keepdims=True)
    o = (o_t * lax.rsqrt(ms + LN_EPS)).T * g_ref[...] * (1.0 - lam_init)
    o_ref[...] = o.astype(o_ref.dtype)


def _diff_attn(hb, bvt, coefs, lam_p, subln_g, lam_init):
    bsz, s, _ = hb.shape
    kb, tq = KEY_BLOCK, DIFF_TQ
    qb, kcol = HB_BQ // LANES, HB_BK // LANES
    kern = functools.partial(_diff_attn_kernel, lam_init=lam_init)
    return pl.pallas_call(
        kern,
        grid=(bsz, B_HEADS, s // tq),
        in_specs=[
            pl.BlockSpec(memory_space=pltpu.SMEM),
            pl.BlockSpec((None, tq, LANES), lambda b, h, i: (b, i, qb + h)),
            pl.BlockSpec((None, s, LANES), lambda b, h, i: (b, 0, kcol + 2 * h)),
            pl.BlockSpec((None, s, LANES), lambda b, h, i: (b, 0, kcol + 2 * h + 1)),
            pl.BlockSpec((None, s // kb, B_VT_ROWS, kb), lambda b, h, i: (b, 0, h, 0)),
            pl.BlockSpec(lam_p.shape, lambda b, h, i: (0, 0)),
            pl.BlockSpec(subln_g.shape, lambda b, h, i: (0, 0)),
        ],
        out_specs=pl.BlockSpec((None, tq, LANES), lambda b, h, i: (b, i, h)),
        out_shape=jax.ShapeDtypeStruct((bsz, s, B_WIDTH), MXU_DTYPE),
        scratch_shapes=[
            pltpu.VMEM((LANES, tq), MXU_DTYPE),
            pltpu.VMEM((LANES, tq), MXU_DTYPE),
            pltpu.VMEM((2, B_VT_ROWS, tq), jnp.float32),
            pltpu.VMEM((2, kb, tq), jnp.float32),
            pltpu.VMEM((2, kb, tq), jnp.float32),
            pltpu.VMEM((2, kb, tq), MXU_DTYPE),
            pltpu.VMEM((2, kb, tq), MXU_DTYPE),
        ],
        compiler_params=pltpu.CompilerParams(
            dimension_semantics=("parallel", "parallel", "arbitrary"),
            vmem_limit_bytes=VMEM_LIMIT),
        name="diff_attn",
    )(coefs, hb, hb, hb, bvt, lam_p, subln_g)


def _dsa_kernel(cq_ref, iq_ref, wq_ref, ck_ref, cvt_ref, ik_ref, o_ref,
                key_ref, planes_ref, qt_ref, wpr_ref, acc_ref, sa_ref, sb_ref, pa_ref, pb_ref,
                *, topk, tq, coefs):
    f32 = jnp.float32
    grp = KEY_BLOCK
    i = pl.program_id(1)
    last = (i * tq) // grp
    n_grp = last + 1

    w_t = wq_ref[...].astype(f32).T[IKW_W_OFF:IKW_W_OFF + IDX_HEADS]
    iq_t = iq_ref[...].astype(f32).T
    zpad = jnp.zeros((LANES - IDX_DIM, tq), f32)
    for hh in range(IDX_HEADS):
        blk = jnp.concatenate([iq_t[hh * IDX_DIM:(hh + 1) * IDX_DIM], zpad], axis=0)
        qt_ref[:, hh * tq:(hh + 1) * tq] = blk.astype(qt_ref.dtype)
    cq_t = cq_ref[...].astype(f32).T
    zhead = jnp.zeros((C_HEAD_DIM, tq), f32)
    frow = lax.broadcasted_iota(jnp.int32, (LANES, tq), 0)
    for pr in range(C_HEADS // 2):
        halves = []
        for hb in range(2):
            hh = 2 * pr + hb
            c_hi, c_lo, _ = coefs[hh]
            qh = cq_t[hh * C_HEAD_DIM:(hh + 1) * C_HEAD_DIM]
            feat = jnp.where(frow == C_FEAT_LANE + 2 * hb, c_hi,
                             jnp.where(frow == C_FEAT_LANE + 2 * hb + 1, c_lo, 0.0))
            halves.append(jnp.concatenate([qh, zhead, feat] if hb == 0 else [zhead, qh, feat], axis=0))
        wpr_ref[pr] = jnp.concatenate(halves, axis=1).astype(wpr_ref.dtype)

    kloc = lax.broadcasted_iota(jnp.int32, (grp, tq), 0)
    qpos = i * tq + lax.broadcasted_iota(jnp.int32, (grp, tq), 1)
    qrow = i * tq + lax.broadcasted_iota(jnp.int32, (1, tq), 1)
    allowed_end = (qrow // CHUNK + 1) * CHUNK

    @pl.when(i == 0)
    def _():
        planes_ref[...] = jnp.zeros(planes_ref.shape, jnp.int32)

    def score_body(g, carry):
        start = pl.multiple_of(g * grp, grp)
        x = _dot(ik_ref[pl.ds(start, grp), :], qt_ref[...])
        sc = jnp.zeros((grp, tq), f32)
        for hh in range(IDX_HEADS):
            sc = sc + w_t[hh:hh + 1] * jnp.maximum(x[:, hh * tq:(hh + 1) * tq], 0.0)
        sc = jnp.where(sc == 0.0, 0.0, sc)
        bits = pltpu.bitcast(sc, jnp.int32)
        key = bits ^ ((bits >> 31) & 0x7FFFFFFF)
        key = jnp.where(g * grp + kloc < allowed_end, key, INT_MIN)
        key_ref[pl.ds(start, grp), :] = key
        w = [key[r * SUBLANES:(r + 1) * SUBLANES, :] for r in range(WORD_BITS)]
        j, msk = 16, 0x0000FFFF
        while j:
            k = 0
            while k < WORD_BITS:
                t = (w[k] ^ (w[k + j] >> j)) & msk
                w[k] = w[k] ^ t
                w[k + j] = w[k + j] ^ (t << j)
                k = (k + j + 1) & ~j
            j >>= 1
            msk = (msk ^ (msk << j)) & 0xFFFFFFFF
        w[0] = ~w[0]
        prow = pl.multiple_of(g * SUBLANES, SUBLANES)
        for r in range(WORD_BITS):
            planes_ref[r, pl.ds(prow, SUBLANES), :] = w[r]
        return carry

    def score_pair(t, carry):
        score_body(2 * t, carry)
        return score_body(2 * t + 1, carry)

    lax.fori_loop(0, n_grp // 2, score_pair, 0)

    @pl.when(n_grp % 2 == 1)
    def _():
        score_body(n_grp - 1, 0)

    nrow = planes_ref.shape[1]
    prow_i = lax.broadcasted_iota(jnp.int32, (nrow, tq), 0)
    pbase = (prow_i >> 3) * grp + (prow_i & 7)

    def top_bits(n):
        n = jnp.clip(n, 0, WORD_BITS)
        return jnp.where(n > 0, lax.shift_left(jnp.int32(-1), (WORD_BITS - n) & (WORD_BITS - 1)), 0)

    def colsum(words):
        c = lax.population_count(words).reshape(nrow // SUBLANES, SUBLANES, tq)
        return jnp.sum(jnp.sum(c, axis=0), axis=0, keepdims=True)

    def bit_body(p, carry):
        cand, above, u = carry
        ones = cand & planes_ref[p]
        c1 = colsum(ones)
        take = (above + c1) >= topk
        cand = jnp.where(take, ones, cand ^ ones)
        above = jnp.where(take, above, above + c1)
        u = u | jnp.where(take, lax.shift_left(jnp.int32(1), WORD_BITS - 1 - p), 0)
        return cand, above, u

    cand0 = top_bits((allowed_end - (prow_i >> 3) * grp) >> 3)
    zero_row = jnp.zeros((1, tq), jnp.int32)
    ties, cnt_gt, u = lax.fori_loop(0, WORD_BITS, bit_body, (cand0, zero_row, zero_row))
    thr = u ^ INT_MIN
    room = topk - cnt_gt

    @pl.when(jnp.max(colsum(ties) - room) > 0)
    def _():
        nbits = max(1, (key_ref.shape[0]).bit_length())

        def tbody(b, c):
            cnd = c | lax.shift_left(jnp.int32(1), nbits - 1 - b)
            cnt = colsum(ties & top_bits((cnd - pbase + 7) >> 3))
            return jnp.where(cnt <= room, cnd, c)

        cut = lax.fori_loop(0, nbits, tbody, zero_row)

        def retire(g, carry):
            start = pl.multiple_of(g * grp, grp)
            keys = key_ref[pl.ds(start, grp), :]
            drop = (keys == thr) & (g * grp + kloc >= cut)
            key_ref[pl.ds(start, grp), :] = jnp.where(drop, INT_MIN, keys)
            return carry

        lax.fori_loop(0, n_grp, retire, 0)

    thr_open = jnp.where(thr > INT_MIN, thr - 1, thr)

    acc_ref[...] = jnp.zeros(acc_ref.shape, f32)
    pb_ref[...] = jnp.zeros(pb_ref.shape, pb_ref.dtype)
    qrow_f = qrow.astype(f32)

    def issue_logits(g, dst_ref):
        start = pl.multiple_of(g * grp, grp)
        for pr in range(C_HEADS // 2):
            dst_ref[pr] = _dot(ck_ref[pl.ds(start, grp), pr * 2 * LANES:(pr + 1) * 2 * LANES], wpr_ref[pr])

    def issue_pv(g, p_ref):
        vt = cvt_ref[g]
        return [_dot(vt[hh * C_VT_ROWS:(hh + 1) * C_VT_ROWS], p_ref[hh]) for hh in range(C_HEADS)]

    def step(g, s_cur, s_nxt, p_cur, p_prev, exact_bias, state):
        pvs = issue_pv(jnp.maximum(g - 1, 0), p_prev)
        if s_nxt is not None:
            issue_logits(g + 1, s_nxt)
        start = pl.multiple_of(g * grp, grp)
        sel = key_ref[pl.ds(start, grp), :] > thr_open
        if exact_bias:
            dist = (jnp.abs(qpos - (g * grp + kloc)) + kloc).astype(f32)
        else:
            base = (g * grp).astype(f32) - qrow_f
        out = []
        for hh in range(C_HEADS):
            c = coefs[hh][2]
            s = s_cur[hh // 2, :, (hh % 2) * tq:(hh % 2 + 1) * tq]
            if exact_bias:
                s = s - c * dist
                coff = jnp.zeros((1, tq), f32)
            else:
                coff = c * base
            s = jnp.where(sel, s, NEG_BIG)
            m_new, alpha, p = _softmax_step(s, coff, state[2 * hh])
            p_cur[hh] = p
            acc_ref[hh] = state[2 * hh + 1] * acc_ref[hh] + pvs[hh]
            out += [m_new, alpha]
        return tuple(out)

    def last_group(s_cur, p_cur, p_prev, state):
        state = step(last, s_cur, None, p_cur, p_prev, True, state)
        pvs = issue_pv(last, p_cur)
        outs = []
        for hh in range(C_HEADS):
            n = state[2 * hh + 1] * acc_ref[hh] + pvs[hh]
            outs.append(n[0:C_HEAD_DIM] / n[C_HEAD_DIM:C_HEAD_DIM + 1])
        return jnp.concatenate(outs, axis=0)

    def pair(t, state):
        state = step(2 * t, sa_ref, sb_ref, pa_ref, pb_ref, False, state)
        return step(2 * t + 1, sb_ref, sa_ref, pb_ref, pa_ref, False, state)

    def odd_tail(state):
        state = step(last - 1, sa_ref, sb_ref, pa_ref, pb_ref, False, state)
        return last_group(sb_ref, pb_ref, pa_ref, state)

    m0 = jnp.full((1, tq), NEG_BIG, f32)
    one = jnp.ones((1, tq), f32)
    issue_logits(0, sa_ref)
    state = lax.fori_loop(0, last // 2, pair, (m0, one) * C_HEADS)
    o_t = lax.cond(last % 2 == 1, odd_tail, lambda st: last_group(sa_ref, pa_ref, pb_ref, st), state)
    o_ref[...] = o_t.T.astype(o_ref.dtype)


def _dsa_attn(hb, cvt, coefs):
    bsz, s, _ = hb.shape
    tq = min(DSA_TQ, s)
    grp = KEY_BLOCK
    topk = min(TOPK_MAX, s // 4)
    cqb, iqb = HB_CQ // C_WIDTH, HB_IQ // C_WIDTH
    ckb, ikwb = HB_CK // (2 * C_WIDTH), HB_IKW // LANES
    kern = functools.partial(_dsa_kernel, topk=topk, tq=tq, coefs=coefs)
    return pl.pallas_call(
        kern,
        grid=(bsz, s // tq),
        in_specs=[
            pl.BlockSpec((None, tq, C_WIDTH), lambda b, i: (b, i, cqb)),
            pl.BlockSpec((None, tq, C_WIDTH), lambda b, i: (b, i, iqb)),
            pl.BlockSpec((None, tq, LANES), lambda b, i: (b, i, ikwb)),
            pl.BlockSpec((None, s, 2 * C_WIDTH), lambda b, i: (b, 0, ckb)),
            pl.BlockSpec((None, s // grp, C_HEADS * C_VT_ROWS, grp), lambda b, i: (b, 0, 0, 0)),
            pl.BlockSpec((None, s, LANES), lambda b, i: (b, 0, ikwb)),
        ],
        out_specs=pl.BlockSpec((None, tq, C_WIDTH), lambda b, i: (b, i, 0)),
        out_shape=jax.ShapeDtypeStruct((bsz, s, C_WIDTH), MXU_DTYPE),
        scratch_shapes=[
            pltpu.VMEM((s, tq), jnp.int32),
            pltpu.VMEM((WORD_BITS, (s // grp) * SUBLANES, tq), jnp.int32),
            pltpu.VMEM((LANES, IDX_HEADS * tq), MXU_DTYPE),
            pltpu.VMEM((C_HEADS // 2, 2 * LANES, 2 * tq), MXU_DTYPE),
            pltpu.VMEM((C_HEADS, C_VT_ROWS, tq), jnp.float32),
            pltpu.VMEM((C_HEADS // 2, grp, 2 * tq), jnp.float32),
            pltpu.VMEM((C_HEADS // 2, grp, 2 * tq), jnp.float32),
            pltpu.VMEM((C_HEADS, grp, tq), MXU_DTYPE),
            pltpu.VMEM((C_HEADS, grp, tq), MXU_DTYPE),
        ],
        compiler_params=pltpu.CompilerParams(
            dimension_semantics=("parallel", "arbitrary"), vmem_limit_bytes=VMEM_LIMIT),
        name="dsa_attn",
    )(hb, hb, hb, hb, cvt, hb)


def _outproj_kernel(oa_ref, ob_ref, oc_ref, x_ref, w_ref, g_ref, b_ref, o_ref, *, alpha):
    y = _dot(oa_ref[...], w_ref[0:A_WIDTH, :])
    y = y + _dot(ob_ref[...], w_ref[A_WIDTH:A_WIDTH + B_WIDTH, :])
    y = y + _dot(oc_ref[...], w_ref[A_WIDTH + B_WIDTH:, :])
    z = alpha * x_ref[...] + y
    o_ref[...] = _layer_norm_rows(z, g_ref[...], b_ref[...])


def _out_proj(oa, ob, oc, x, w, g, b, alpha):
    bsz, s, d = x.shape
    tm = min(ROW_TILE, s)
    row = lambda width: pl.BlockSpec((None, tm, width), lambda bb, i: (bb, i, 0))
    full = lambda a: pl.BlockSpec(a.shape, lambda bb, i: (0, 0))
    return pl.pallas_call(
        functools.partial(_outproj_kernel, alpha=alpha),
        grid=(bsz, s // tm),
        in_specs=[row(A_WIDTH), row(B_WIDTH), row(C_WIDTH), row(d), full(w), full(g), full(b)],
        out_specs=row(d),
        out_shape=jax.ShapeDtypeStruct((bsz, s, d), jnp.float32),
        compiler_params=pltpu.CompilerParams(
            dimension_semantics=("parallel", "parallel"), vmem_limit_bytes=VMEM_LIMIT),
        name="out_proj",
    )(oa, ob, oc, x, w, g, b)


def _ffn_kernel(x_ref, wg_ref, wu_ref, wd_ref, g_ref, b_ref, o_ref, acc_ref, *, alpha, fc):
    x = x_ref[...]
    xb = x.astype(MXU_DTYPE)
    hidden = wg_ref.shape[1]
    for c0 in range(0, hidden, fc):
        gate = _dot(xb, wg_ref[:, c0:c0 + fc])
        up = _dot(xb, wu_ref[:, c0:c0 + fc])
        hid = (jax.nn.silu(gate) * up).astype(MXU_DTYPE)
        part = _dot(hid, wd_ref[c0:c0 + fc, :])
        if c0 == 0:
            acc_ref[...] = part
        else:
            acc_ref[...] += part
    z = alpha * x + acc_ref[...]
    o_ref[...] = _layer_norm_rows(z, g_ref[...], b_ref[...])


def _ffn(x, wg, wu, wd, g, b, alpha):
    bsz, s, d = x.shape
    tm = min(ROW_TILE, s)
    row = pl.BlockSpec((None, tm, d), lambda bb, i: (bb, i, 0))
    full = lambda a: pl.BlockSpec(a.shape, lambda bb, i: (0, 0))
    return pl.pallas_call(
        functools.partial(_ffn_kernel, alpha=alpha, fc=256),
        grid=(bsz, s // tm),
        in_specs=[row, full(wg), full(wu), full(wd), full(g), full(b)],
        out_specs=row,
        out_shape=jax.ShapeDtypeStruct((bsz, s, d), jnp.float32),
        scratch_shapes=[pltpu.VMEM((tm, d), jnp.float32)],
        compiler_params=pltpu.CompilerParams(
            dimension_semantics=("parallel", "parallel"), vmem_limit_bytes=VMEM_LIMIT),
        name="ffn",
    )(x, wg, wu, wd, g, b)


def _prep_in_weights(w):
    d = w.shape[0]
    off_bq = 2 * A_WIDTH
    off_bk = off_bq + B_WIDTH
    off_bv = off_bk + B_WIDTH
    off_cq = off_bv + B_WIDTH
    off_ck = off_cq + C_WIDTH
    off_cv = off_ck + C_WIDTH
    off_iq = off_cv + C_WIDTH
    off_ik = off_iq + IDX_HEADS * IDX_DIM
    off_iw = off_ik + IDX_DIM
    q_scale = (B_QK_DIM ** -0.5) * LOG2E
    c_scale = (C_HEAD_DIM ** -0.5) * LOG2E
    i_scale = (IDX_HEADS ** -0.5) * (IDX_DIM ** -0.5)
    zeros = lambda n: jnp.zeros((d, n), w.dtype)
    cols = [w[:, off_bq:off_bk] * q_scale]
    bk = w[:, off_bk:off_bv]
    for h in range(B_HEADS):
        for m in range(2):
            c0 = (2 * h + m) * B_QK_DIM
            cols += [bk[:, c0:c0 + B_QK_DIM], zeros(LANES - B_QK_DIM)]
    ck = w[:, off_ck:off_cv]
    for pr in range(C_HEADS // 2):
        cols += [ck[:, pr * LANES:(pr + 1) * LANES], zeros(LANES)]
    cols += [w[:, off_cq:off_ck] * c_scale, w[:, off_iq:off_iw],
             w[:, off_iw:off_iw + IDX_HEADS] * i_scale, zeros(LANES - IDX_DIM - IDX_HEADS)]
    wb = jnp.concatenate(cols, axis=1)
    wv = jnp.concatenate([w[:, off_bv:off_cq], w[:, off_cv:off_iq]], axis=1)
    return w[:, :off_bq].astype(MXU_DTYPE), wb.astype(MXU_DTYPE), wv.astype(MXU_DTYPE)


def kernel(x, w_in, gmlp_w_s, gmlp_b_s, gmlp_ln_g, gmlp_ln_b, lam_q1, lam_k1, lam_q2, lam_k2,
           diff_subln_g, w_out, ln1_g, ln1_b, w_gu, w_down, ln2_g, ln2_b):
    depth = w_in.shape[0]
    alpha = (2 * depth) ** 0.25
    hidden = w_down.shape[1]
    coefs_b = _alibi_coefs(B_HEADS)
    coefs_c = tuple(_alibi_coefs(C_HEADS))
    coef_tab = jnp.asarray([v for hi, lo, c in coefs_b for v in (hi, lo, c, 0.0)], jnp.float32)
    for l in range(depth):
        lam_init = 0.8 - 0.6 * math.exp(-0.3 * l)
        wa, wb, wv = _prep_in_weights(w_in[l])
        ha, hb, bvt, cvt = _in_proj(x, wa, wb, wv)
        out_a = _gmlp(ha, gmlp_w_s[l], jnp.repeat(gmlp_b_s[l].T, A_GDIM, axis=1),
                      gmlp_ln_g[l].reshape(1, A_WIDTH), gmlp_ln_b[l].reshape(1, A_WIDTH))
        lam_p = jnp.stack([lam_q1[l], lam_k1[l], lam_q2[l], lam_k2[l]]).astype(jnp.float32)
        out_b = _diff_attn(hb, bvt, coef_tab, lam_p, diff_subln_g[l].reshape(1, B_V_DIM), lam_init)
        out_c = _dsa_attn(hb, cvt, coefs_c)
        x = _out_proj(out_a, out_b, out_c, x, w_out[l].astype(MXU_DTYPE),
                      ln1_g[l].reshape(1, -1), ln1_b[l].reshape(1, -1), alpha)
        x = _ffn(x, w_gu[l, :, :hidden].astype(MXU_DTYPE), w_gu[l, :, hidden:].astype(MXU_DTYPE),
                 w_down[l].astype(MXU_DTYPE), ln2_g[l].reshape(1, -1), ln2_b[l].reshape(1, -1), alpha)
    return x
```

```python
import functools
import math
import struct

import jax
import jax.numpy as jnp
from jax import lax
from jax.experimental import pallas as pl
from jax.experimental.pallas import tpu as pltpu

CHUNK = 64
GMLP_BLOCK = 128
A_GROUPS, A_GDIM = 4, 64
A_WIDTH = A_GROUPS * A_GDIM
B_HEADS, B_QK_DIM = 4, 64
B_V_DIM = 2 * B_QK_DIM
B_WIDTH = B_HEADS * B_V_DIM
C_HEADS, C_HEAD_DIM = 4, 64
C_WIDTH = C_HEADS * C_HEAD_DIM
IDX_HEADS, IDX_DIM = 8, 32
TOPK_MAX = 256
LN_EPS = 1e-5
LOG2E = 1.4426950408889634

LANES = 128
SUBLANES = 8
WORD_BITS = 32
KEY_BLOCK = 256
V_PAD = 16

HB_BQ = 0
HB_BK = HB_BQ + B_WIDTH
HB_CK = HB_BK + 2 * B_WIDTH
HB_CQ = HB_CK + 2 * C_WIDTH
HB_IQ = HB_CQ + C_WIDTH
HB_IKW = HB_IQ + IDX_HEADS * IDX_DIM
HB_WIDTH = HB_IKW + LANES
IKW_W_OFF = IDX_DIM
C_FEAT_LANE = 0
WB_BQ = 0
WB_BK = WB_BQ + B_WIDTH
WB_CK = WB_BK + B_WIDTH
WB_REST = WB_CK + C_WIDTH

MXU_DTYPE = jnp.bfloat16
NEG_BIG = -1e30
B_VT_ROWS = B_V_DIM + V_PAD
C_VT_ROWS = C_HEAD_DIM + V_PAD
INT_MIN = -(2 ** 31)

ROW_TILE = 512
DSA_TQ = 256
DIFF_TQ = 2 * KEY_BLOCK
VMEM_LIMIT = 56 * 1024 * 1024


def _dot(a, b):
    return jnp.dot(a, b, preferred_element_type=jnp.float32)


def _layer_norm_rows(z, g, b):
    mu = jnp.mean(z, axis=-1, keepdims=True)
    zc = z - mu
    var = jnp.mean(zc * zc, axis=-1, keepdims=True)
    return zc * lax.rsqrt(var + LN_EPS) * g + b


def _bf16_round(v):
    bits = struct.unpack("<I", struct.pack("<f", v))[0]
    bits = ((bits + 0x7FFF + ((bits >> 16) & 1)) >> 16) << 16
    return struct.unpack("<f", struct.pack("<I", bits & 0xFFFFFFFF))[0]


def _alibi_coefs(n):
    out = []
    for h in range(n):
        c = (2.0 ** (-8.0 * (h + 1) / n)) * LOG2E
        hi = _bf16_round(c)
        out.append((hi, c - hi, c))
    return out


def _inproj_kernel(x_ref, wa_ref, wb_ref, wv_ref, ha_ref, hb_ref, bvt_ref, cvt_ref):
    tm = x_ref.shape[0]
    xb = x_ref[...].astype(MXU_DTYPE)
    ha_ref[...] = _dot(xb, wa_ref[...])

    kloc = (lax.broadcasted_iota(jnp.int32, (tm, LANES), 0) % KEY_BLOCK).astype(jnp.float32)
    lane = lax.broadcasted_iota(jnp.int32, (tm, LANES), 1)
    low = lane < B_QK_DIM
    feat_k1 = jnp.where((lane == B_QK_DIM) | (lane == B_QK_DIM + 1), kloc, 0.0)
    feat_k2 = jnp.where(lane < 2, kloc, 0.0)
    feat_c = jnp.where(lane < 4, kloc, 0.0)

    hb_ref[:, HB_BQ:HB_BK] = _dot(xb, wb_ref[:, WB_BQ:WB_BK]).astype(hb_ref.dtype)
    yk = _dot(xb, wb_ref[:, WB_BK:WB_CK])
    blocks = []
    for h in range(B_HEADS):
        y = yk[:, h * LANES:(h + 1) * LANES]
        blocks += [jnp.where(low, y, feat_k1), jnp.where(low, feat_k2, y)]
    hb_ref[:, HB_BK:HB_CK] = jnp.concatenate(blocks, axis=1).astype(hb_ref.dtype)
    yc = _dot(xb, wb_ref[:, WB_CK:WB_REST])
    blocks = []
    for pr in range(C_HEADS // 2):
        blocks += [yc[:, pr * LANES:(pr + 1) * LANES], feat_c]
    hb_ref[:, HB_CK:HB_CQ] = jnp.concatenate(blocks, axis=1).astype(hb_ref.dtype)
    hb_ref[:, HB_CQ:HB_WIDTH] = _dot(xb, wb_ref[:, WB_REST:]).astype(hb_ref.dtype)

    yv = _dot(xb, wv_ref[...])
    ones_blk = jnp.where(lax.broadcasted_iota(jnp.int32, (V_PAD, KEY_BLOCK), 0) == 0, 1.0, 0.0)

    def heads_t(y, dim):
        y_t = y.T
        parts = []
        for h in range(y.shape[1] // dim):
            parts += [y_t[h * dim:(h + 1) * dim], ones_blk]
        return jnp.concatenate(parts, axis=0)

    for g in range(tm // KEY_BLOCK):
        rows = slice(g * KEY_BLOCK, (g + 1) * KEY_BLOCK)
        bvt_ref[g] = heads_t(yv[rows, 0:B_WIDTH], B_V_DIM).astype(bvt_ref.dtype)
        cvt_ref[g] = heads_t(yv[rows, B_WIDTH:], C_HEAD_DIM).astype(cvt_ref.dtype)


def _in_proj(x, wa, wb, wv):
    bsz, s, d = x.shape
    tm = min(ROW_TILE, s)
    gpt = tm // KEY_BLOCK
    full = lambda a: pl.BlockSpec(a.shape, lambda b, i: (0, 0))
    return pl.pallas_call(
        _inproj_kernel,
        grid=(bsz, s // tm),
        in_specs=[pl.BlockSpec((None, tm, d), lambda b, i: (b, i, 0)), full(wa), full(wb), full(wv)],
        out_specs=[
            pl.BlockSpec((None, tm, 2 * A_WIDTH), lambda b, i: (b, i, 0)),
            pl.BlockSpec((None, tm, HB_WIDTH), lambda b, i: (b, i, 0)),
            pl.BlockSpec((None, gpt, B_HEADS * B_VT_ROWS, KEY_BLOCK), lambda b, i: (b, i, 0, 0)),
            pl.BlockSpec((None, gpt, C_HEADS * C_VT_ROWS, KEY_BLOCK), lambda b, i: (b, i, 0, 0)),
        ],
        out_shape=[
            jax.ShapeDtypeStruct((bsz, s, 2 * A_WIDTH), jnp.float32),
            jax.ShapeDtypeStruct((bsz, s, HB_WIDTH), MXU_DTYPE),
            jax.ShapeDtypeStruct((bsz, s // KEY_BLOCK, B_HEADS * B_VT_ROWS, KEY_BLOCK), MXU_DTYPE),
            jax.ShapeDtypeStruct((bsz, s // KEY_BLOCK, C_HEADS * C_VT_ROWS, KEY_BLOCK), MXU_DTYPE),
        ],
        compiler_params=pltpu.CompilerParams(
            dimension_semantics=("parallel", "parallel"), vmem_limit_bytes=VMEM_LIMIT),
        name="in_proj",
    )(x, wa, wb, wv)


def _gmlp_kernel(ha_ref, ws_ref, bias_ref, g_ref, b_ref, o_ref):
    f32 = jnp.float32
    nblk = ha_ref.shape[0] // GMLP_BLOCK
    r = lax.broadcasted_iota(jnp.int32, (GMLP_BLOCK, GMLP_BLOCK), 0) // CHUNK
    c = lax.broadcasted_iota(jnp.int32, (GMLP_BLOCK, GMLP_BLOCK), 1) // CHUNK
    w_all = jnp.concatenate([jnp.where(c <= r, ws_ref[g], 0.0) for g in range(A_GROUPS)],
                            axis=0).astype(MXU_DTYPE)
    ar = lax.broadcasted_iota(jnp.int32, (A_WIDTH, A_WIDTH), 0) // A_GDIM
    ac = lax.broadcasted_iota(jnp.int32, (A_WIDTH, A_WIDTH), 1) // A_GDIM
    avg = jnp.where(ar == ac, 1.0 / A_GDIM, 0.0).astype(MXU_DTYPE)
    lane_grp = lax.broadcasted_iota(jnp.int32, (GMLP_BLOCK, A_WIDTH), 1) // A_GDIM

    def group_mean(z):
        hi = z.astype(MXU_DTYPE)
        lo = (z - hi.astype(f32)).astype(MXU_DTYPE)
        return _dot(hi, avg) + _dot(lo, avg)

    u = jax.nn.gelu(ha_ref[:, 0:A_WIDTH])
    v = jax.nn.gelu(ha_ref[:, A_WIDTH:2 * A_WIDTH])
    vc = v - group_mean(v)
    vn = (vc * lax.rsqrt(group_mean(vc * vc) + LN_EPS) * g_ref[...] + b_ref[...]).astype(MXU_DTYPE)
    for blk in range(nblk):
        rows = slice(blk * GMLP_BLOCK, (blk + 1) * GMLP_BLOCK)
        z = _dot(w_all, vn[rows, :])
        sg = bias_ref[...]
        for g in range(A_GROUPS):
            sg = sg + jnp.where(lane_grp == g, z[g * GMLP_BLOCK:(g + 1) * GMLP_BLOCK], 0.0)
        o_ref[rows, :] = (u[rows, :] * sg).astype(o_ref.dtype)


def _gmlp(ha, w_s, b_s_t, ln_g, ln_b):
    bsz, s, _ = ha.shape
    tm = min(ROW_TILE, s)
    return pl.pallas_call(
        _gmlp_kernel,
        grid=(bsz, s // tm),
        in_specs=[
            pl.BlockSpec((None, tm, 2 * A_WIDTH), lambda b, i: (b, i, 0)),
            pl.BlockSpec(w_s.shape, lambda b, i: (0, 0, 0)),
            pl.BlockSpec(b_s_t.shape, lambda b, i: (0, 0)),
            pl.BlockSpec(ln_g.shape, lambda b, i: (0, 0)),
            pl.BlockSpec(ln_b.shape, lambda b, i: (0, 0)),
        ],
        out_specs=pl.BlockSpec((None, tm, A_WIDTH), lambda b, i: (b, i, 0)),
        out_shape=jax.ShapeDtypeStruct((bsz, s, A_WIDTH), MXU_DTYPE),
        compiler_params=pltpu.CompilerParams(
            dimension_semantics=("parallel", "parallel"), vmem_limit_bytes=VMEM_LIMIT),
        name="gmlp",
    )(ha, w_s, b_s_t, ln_g, ln_b)


def _softmax_step(s, coff, m):
    m_new = jnp.maximum(m, jnp.max(s, axis=0, keepdims=True) + coff)
    p = jnp.exp2(s - (m_new - coff)).astype(MXU_DTYPE)
    return m_new, jnp.exp2(m - m_new), p


def _diff_attn_kernel(coef_ref, q_ref, k1_ref, k2_ref, vt_ref, lam_ref, g_ref, o_ref,
                      w1_ref, w2_ref, acc_ref, sa_ref, sb_ref, pa_ref, pb_ref, *, lam_init):
    f32 = jnp.float32
    kb = KEY_BLOCK
    tq = DIFF_TQ
    h = pl.program_id(1)
    i = pl.program_id(2)
    c_hi, c_lo, c = coef_ref[4 * h], coef_ref[4 * h + 1], coef_ref[4 * h + 2]

    q_t = q_ref[...].astype(f32).T
    frow = lax.broadcasted_iota(jnp.int32, (LANES - B_QK_DIM, tq), 0)
    feat = jnp.where(frow == 0, c_hi, jnp.where(frow == 1, c_lo, 0.0))
    w1_ref[...] = jnp.concatenate([q_t[0:B_QK_DIM], feat], axis=0).astype(w1_ref.dtype)
    w2_ref[...] = jnp.concatenate([feat, q_t[B_QK_DIM:]], axis=0).astype(w2_ref.dtype)
    acc_ref[...] = jnp.zeros(acc_ref.shape, f32)
    pb_ref[...] = jnp.zeros(pb_ref.shape, pb_ref.dtype)
    qrow = (i * tq + lax.broadcasted_iota(jnp.int32, (1, tq), 1)).astype(f32)

    def issue_logits(j, dst_ref):
        start = pl.multiple_of(j * kb, kb)
        dst_ref[0] = _dot(k1_ref[pl.ds(start, kb), :], w1_ref[...])
        dst_ref[1] = _dot(k2_ref[pl.ds(start, kb), :], w2_ref[...])

    def issue_pv(j, p_ref):
        vt = vt_ref[j]
        return _dot(vt, p_ref[0]), _dot(vt, p_ref[1])

    def step(j, s_cur, s_nxt, p_cur, p_prev, fix, coff, state):
        m1, m2, a1_prev, a2_prev = state
        s1, s2 = s_cur[0], s_cur[1]
        if fix is not None:
            s1, s2 = s1 + fix, s2 + fix
        m1, a1, p1 = _softmax_step(s1, coff, m1)
        p_cur[0] = p1
        pv1, pv2 = issue_pv(jnp.maximum(j - 1, 0), p_prev)
        m2, a2, p2 = _softmax_step(s2, coff, m2)
        p_cur[1] = p2
        if s_nxt is not None:
            issue_logits(j + 1, s_nxt)
        acc_ref[0] = a1_prev * acc_ref[0] + pv1
        acc_ref[1] = a2_prev * acc_ref[1] + pv2
        return m1, m2, a1, a2

    def past_block(j, s_cur, s_nxt, p_cur, p_prev, state):
        coff = c * ((j * kb).astype(f32) - qrow)
        return step(j, s_cur, s_nxt, p_cur, p_prev, None, coff, state)

    def pair(t, state):
        state = past_block(2 * t, sa_ref, sb_ref, pa_ref, pb_ref, state)
        return past_block(2 * t + 1, sb_ref, sa_ref, pb_ref, pa_ref, state)

    m0 = jnp.full((1, tq), NEG_BIG, f32)
    one = jnp.ones((1, tq), f32)
    issue_logits(0, sa_ref)
    state = lax.fori_loop(0, i, pair, (m0, m0, one, one))

    kloc = lax.broadcasted_iota(jnp.int32, (kb, kb), 0)
    qloc = lax.broadcasted_iota(jnp.int32, (kb, kb), 1)
    diag = -c * (jnp.abs(qloc - kloc) + kloc).astype(f32)
    diag = jnp.where((kloc // CHUNK) <= (qloc // CHUNK), diag, NEG_BIG)
    zero_row = jnp.zeros((1, kb), f32)
    lo = 2 * i
    fix = jnp.concatenate([diag, jnp.zeros((kb, kb), f32)], axis=1)
    coff = jnp.concatenate([zero_row, c * ((lo * kb).astype(f32) - qrow[:, kb:])], axis=1)
    state = step(lo, sa_ref, sb_ref, pa_ref, pb_ref, fix, coff, state)
    fix = jnp.concatenate([jnp.full((kb, kb), NEG_BIG, f32), diag], axis=1)
    _, _, a1, a2 = step(lo + 1, sb_ref, None, pb_ref, pa_ref, fix, jnp.zeros((1, tq), f32), state)
    pv1, pv2 = issue_pv(lo + 1, pb_ref)
    n1 = a1 * acc_ref[0] + pv1
    n2 = a2 * acc_ref[1] + pv2
    o1 = n1[0:B_V_DIM] / n1[B_V_DIM:B_V_DIM + 1]
    o2 = n2[0:B_V_DIM] / n2[B_V_DIM:B_V_DIM + 1]

    lam_p = lam_ref[...]
    e1 = jnp.exp(jnp.sum(lam_p[0:1] * lam_p[1:2], axis=-1, keepdims=True))
    e2 = jnp.exp(jnp.sum(lam_p[2:3] * lam_p[3:4], axis=-1, keepdims=True))
    lam = e1 - e2 + lam_init
    o_t = o1 - lam * o2
    ms = jnp.mean(o_t * o_t, axis=0, keepdims=True)
    o = (o_t * lax.rsqrt(ms + LN_EPS)).T * g_ref[...] * (1.0 - lam_init)
    o_ref[...] = o.astype(o_ref.dtype)


def _diff_attn(hb, bvt, coefs, lam_p, subln_g, lam_init):
    bsz, s, _ = hb.shape
    kb, tq = KEY_BLOCK, DIFF_TQ
    qb, kcol = HB_BQ // LANES, HB_BK // LANES
    kern = functools.partial(_diff_attn_kernel, lam_init=lam_init)
    return pl.pallas_call(
        kern,
        grid=(bsz, B_HEADS, s // tq),
        in_specs=[
            pl.BlockSpec(memory_space=pltpu.SMEM),
            pl.BlockSpec((None, tq, LANES), lambda b, h, i: (b, i, qb + h)),
            pl.BlockSpec((None, s, LANES), lambda b, h, i: (b, 0, kcol + 2 * h)),
            pl.BlockSpec((None, s, LANES), lambda b, h, i: (b, 0, kcol + 2 * h + 1)),
            pl.BlockSpec((None, s // kb, B_VT_ROWS, kb), lambda b, h, i: (b, 0, h, 0)),
            pl.BlockSpec(lam_p.shape, lambda b, h, i: (0, 0)),
            pl.BlockSpec(subln_g.shape, lambda b, h, i: (0, 0)),
        ],
        out_specs=pl.BlockSpec((None, tq, LANES), lambda b, h, i: (b, i, h)),
        out_shape=jax.ShapeDtypeStruct((bsz, s, B_WIDTH), MXU_DTYPE),
        scratch_shapes=[
            pltpu.VMEM((LANES, tq), MXU_DTYPE),
            pltpu.VMEM((LANES, tq), MXU_DTYPE),
            pltpu.VMEM((2, B_VT_ROWS, tq), jnp.float32),
            pltpu.VMEM((2, kb, tq), jnp.float32),
            pltpu.VMEM((2, kb, tq), jnp.float32),
            pltpu.VMEM((2, kb, tq), MXU_DTYPE),
            pltpu.VMEM((2, kb, tq), MXU_DTYPE),
        ],
        compiler_params=pltpu.CompilerParams(
            dimension_semantics=("parallel", "parallel", "arbitrary"),
            vmem_limit_bytes=VMEM_LIMIT),
        name="diff_attn",
    )(coefs, hb, hb, hb, bvt, lam_p, subln_g)


def _dsa_kernel(cq_ref, iq_ref, wq_ref, ck_ref, cvt_ref, ik_ref, o_ref,
                key_ref, planes_ref, qt_ref, wpr_ref, acc_ref, sa_ref, sb_ref, pa_ref, pb_ref,
                *, topk, tq, coefs):
    f32 = jnp.float32
    grp = KEY_BLOCK
    i = pl.program_id(1)
    last = (i * tq) // grp
    n_grp = last + 1

    w_t = wq_ref[...].astype(f32).T[IKW_W_OFF:IKW_W_OFF + IDX_HEADS]
    iq_t = iq_ref[...].astype(f32).T
    zpad = jnp.zeros((LANES - IDX_DIM, tq), f32)
    for hh in range(IDX_HEADS):
        blk = jnp.concatenate([iq_t[hh * IDX_DIM:(hh + 1) * IDX_DIM], zpad], axis=0)
        qt_ref[:, hh * tq:(hh + 1) * tq] = blk.astype(qt_ref.dtype)
    cq_t = cq_ref[...].astype(f32).T
    zhead = jnp.zeros((C_HEAD_DIM, tq), f32)
    frow = lax.broadcasted_iota(jnp.int32, (LANES, tq), 0)
    for pr in range(C_HEADS // 2):
        halves = []
        for hb in range(2):
            hh = 2 * pr + hb
            c_hi, c_lo, _ = coefs[hh]
            qh = cq_t[hh * C_HEAD_DIM:(hh + 1) * C_HEAD_DIM]
            feat = jnp.where(frow == C_FEAT_LANE + 2 * hb, c_hi,
                             jnp.where(frow == C_FEAT_LANE + 2 * hb + 1, c_lo, 0.0))
            halves.append(jnp.concatenate([qh, zhead, feat] if hb == 0 else [zhead, qh, feat], axis=0))
        wpr_ref[pr] = jnp.concatenate(halves, axis=1).astype(wpr_ref.dtype)

    kloc = lax.broadcasted_iota(jnp.int32, (grp, tq), 0)
    qpos = i * tq + lax.broadcasted_iota(jnp.int32, (grp, tq), 1)
    qrow = i * tq + lax.broadcasted_iota(jnp.int32, (1, tq), 1)
    allowed_end = (qrow // CHUNK + 1) * CHUNK

    @pl.when(i == 0)
    def _():
        planes_ref[...] = jnp.zeros(planes_ref.shape, jnp.int32)

    def score_body(g, masked):
        start = pl.multiple_of(g * grp, grp)
        x = _dot(ik_ref[pl.ds(start, grp), :], qt_ref[...])
        sc = w_t[0:1] * jnp.maximum(x[:, 0:tq], 0.0)
        for hh in range(1, IDX_HEADS):
            sc = sc + w_t[hh:hh + 1] * jnp.maximum(x[:, hh * tq:(hh + 1) * tq], 0.0)
        bits = pltpu.bitcast(sc, jnp.int32)
        sign = bits >> 31
        key = ((bits & 0x7FFFFFFF) ^ sign) - sign
        if masked:
            key = jnp.where(g * grp + kloc < allowed_end, key, INT_MIN)
        key_ref[pl.ds(start, grp), :] = key
        w = [key[r * SUBLANES:(r + 1) * SUBLANES, :] for r in range(WORD_BITS)]
        j, msk = 16, 0x0000FFFF
        while j:
            k = 0
            while k < WORD_BITS:
                t = (w[k] ^ (w[k + j] >> j)) & msk
                w[k] = w[k] ^ t
                w[k + j] = w[k + j] ^ (t << j)
                k = (k + j + 1) & ~j
            j >>= 1
            msk = (msk ^ (msk << j)) & 0xFFFFFFFF
        w[0] = ~w[0]
        prow = pl.multiple_of(g * SUBLANES, SUBLANES)
        for r in range(WORD_BITS):
            planes_ref[r, pl.ds(prow, SUBLANES), :] = w[r]

    def score_pair(t, carry):
        score_body(2 * t, False)
        score_body(2 * t + 1, False)
        return carry

    lax.fori_loop(0, last // 2, score_pair, 0)

    @pl.when(last % 2 == 1)
    def _():
        score_body(last - 1, False)

    score_body(last, True)

    nrow = planes_ref.shape[1]
    prow_i = lax.broadcasted_iota(jnp.int32, (nrow, tq), 0)
    pbase = (prow_i >> 3) * grp + (prow_i & 7)

    def top_bits(n):
        n = jnp.clip(n, 0, WORD_BITS)
        return jnp.where(n > 0, lax.shift_left(jnp.int32(-1), (WORD_BITS - n) & (WORD_BITS - 1)), 0)

    def colsum(words):
        c = lax.population_count(words).reshape(nrow // SUBLANES, SUBLANES, tq)
        return jnp.sum(jnp.sum(c, axis=0), axis=0, keepdims=True)

    def bit_body(p, carry):
        cand, above, u = carry
        ones = cand & planes_ref[p]
        c1 = colsum(ones)
        take = (above + c1) >= topk
        cand = jnp.where(take, ones, cand ^ ones)
        above = jnp.where(take, above, above + c1)
        u = u | jnp.where(take, lax.shift_left(jnp.int32(1), WORD_BITS - 1 - p), 0)
        return cand, above, u

    cand0 = top_bits((allowed_end - (prow_i >> 3) * grp) >> 3)
    zero_row = jnp.zeros((1, tq), jnp.int32)
    ties, cnt_gt, u = lax.fori_loop(0, WORD_BITS, bit_body, (cand0, zero_row, zero_row))
    thr = u ^ INT_MIN
    room = topk - cnt_gt

    @pl.when(jnp.max(colsum(ties) - room) > 0)
    def _():
        nbits = max(1, (key_ref.shape[0]).bit_length())

        def tbody(b, c):
            cnd = c | lax.shift_left(jnp.int32(1), nbits - 1 - b)
            cnt = colsum(ties & top_bits((cnd - pbase + 7) >> 3))
            return jnp.where(cnt <= room, cnd, c)

        cut = lax.fori_loop(0, nbits, tbody, zero_row)

        def retire(g, carry):
            start = pl.multiple_of(g * grp, grp)
            keys = key_ref[pl.ds(start, grp), :]
            drop = (keys == thr) & (g * grp + kloc >= cut)
            key_ref[pl.ds(start, grp), :] = jnp.where(drop, INT_MIN, keys)
            return carry

        lax.fori_loop(0, n_grp, retire, 0)

    thr_open = jnp.where(thr > INT_MIN, thr - 1, thr)

    acc_ref[...] = jnp.zeros(acc_ref.shape, f32)
    pb_ref[...] = jnp.zeros(pb_ref.shape, pb_ref.dtype)
    qrow_f = qrow.astype(f32)

    def issue_logits(g, dst_ref):
        start = pl.multiple_of(g * grp, grp)
        for pr in range(C_HEADS // 2):
            dst_ref[pr] = _dot(ck_ref[pl.ds(start, grp), pr * 2 * LANES:(pr + 1) * 2 * LANES], wpr_ref[pr])

    def issue_pv(g, p_ref):
        vt = cvt_ref[g]
        return [_dot(vt[hh * C_VT_ROWS:(hh + 1) * C_VT_ROWS], p_ref[hh]) for hh in range(C_HEADS)]

    def step(g, s_cur, s_nxt, p_cur, p_prev, exact_bias, state):
        pvs = issue_pv(jnp.maximum(g - 1, 0), p_prev)
        start = pl.multiple_of(g * grp, grp)
        sel = key_ref[pl.ds(start, grp), :] > thr_open
        if exact_bias:
            dist = (jnp.abs(qpos - (g * grp + kloc)) + kloc).astype(f32)
        else:
            base = (g * grp).astype(f32) - qrow_f
        out = []
        for hh in range(C_HEADS):
            c = coefs[hh][2]
            s = s_cur[hh // 2, :, (hh % 2) * tq:(hh % 2 + 1) * tq]
            if exact_bias:
                s = s - c * dist
                coff = jnp.zeros((1, tq), f32)
            else:
                coff = c * base
            s = jnp.where(sel, s, NEG_BIG)
            m_new, alpha, p = _softmax_step(s, coff, state[2 * hh])
            p_cur[hh] = p
            out += [m_new, alpha]
        if s_nxt is not None:
            issue_logits(g + 1, s_nxt)
        for hh in range(C_HEADS):
            acc_ref[hh] = state[2 * hh + 1] * acc_ref[hh] + pvs[hh]
        return tuple(out)

    def last_group(s_cur, p_cur, p_prev, state):
        state = step(last, s_cur, None, p_cur, p_prev, True, state)
        pvs = issue_pv(last, p_cur)
        outs = []
        for hh in range(C_HEADS):
            n = state[2 * hh + 1] * acc_ref[hh] + pvs[hh]
            outs.append(n[0:C_HEAD_DIM] / n[C_HEAD_DIM:C_HEAD_DIM + 1])
        return jnp.concatenate(outs, axis=0)

    def pair(t, state):
        state = step(2 * t, sa_ref, sb_ref, pa_ref, pb_ref, False, state)
        return step(2 * t + 1, sb_ref, sa_ref, pb_ref, pa_ref, False, state)

    def odd_tail(state):
        state = step(last - 1, sa_ref, sb_ref, pa_ref, pb_ref, False, state)
        return last_group(sb_ref, pb_ref, pa_ref, state)

    m0 = jnp.full((1, tq), NEG_BIG, f32)
    one = jnp.ones((1, tq), f32)
    issue_logits(0, sa_ref)
    state = lax.fori_loop(0, last // 2, pair, (m0, one) * C_HEADS)
    o_t = lax.cond(last % 2 == 1, odd_tail, lambda st: last_group(sa_ref, pa_ref, pb_ref, st), state)
    o_ref[...] = o_t.T.astype(o_ref.dtype)


def _dsa_attn(hb, cvt, coefs):
    bsz, s, _ = hb.shape
    tq = min(DSA_TQ, s)
    grp = KEY_BLOCK
    topk = min(TOPK_MAX, s // 4)
    cqb, iqb = HB_CQ // C_WIDTH, HB_IQ // C_WIDTH
    ckb, ikwb = HB_CK // (2 * C_WIDTH), HB_IKW // LANES
    kern = functools.partial(_dsa_kernel, topk=topk, tq=tq, coefs=coefs)
    return pl.pallas_call(
        kern,
        grid=(bsz, s // tq),
        in_specs=[
            pl.BlockSpec((None, tq, C_WIDTH), lambda b, i: (b, i, cqb)),
            pl.BlockSpec((None, tq, C_WIDTH), lambda b, i: (b, i, iqb)),
            pl.BlockSpec((None, tq, LANES), lambda b, i: (b, i, ikwb)),
            pl.BlockSpec((None, s, 2 * C_WIDTH), lambda b, i: (b, 0, ckb)),
            pl.BlockSpec((None, s // grp, C_HEADS * C_VT_ROWS, grp), lambda b, i: (b, 0, 0, 0)),
            pl.BlockSpec((None, s, LANES), lambda b, i: (b, 0, ikwb)),
        ],
        out_specs=pl.BlockSpec((None, tq, C_WIDTH), lambda b, i: (b, i, 0)),
        out_shape=jax.ShapeDtypeStruct((bsz, s, C_WIDTH), MXU_DTYPE),
        scratch_shapes=[
            pltpu.VMEM((s, tq), jnp.int32),
            pltpu.VMEM((WORD_BITS, (s // grp) * SUBLANES, tq), jnp.int32),
            pltpu.VMEM((LANES, IDX_HEADS * tq), MXU_DTYPE),
            pltpu.VMEM((C_HEADS // 2, 2 * LANES, 2 * tq), MXU_DTYPE),
            pltpu.VMEM((C_HEADS, C_VT_ROWS, tq), jnp.float32),
            pltpu.VMEM((C_HEADS // 2, grp, 2 * tq), jnp.float32),
            pltpu.VMEM((C_HEADS // 2, grp, 2 * tq), jnp.float32),
            pltpu.VMEM((C_HEADS, grp, tq), MXU_DTYPE),
            pltpu.VMEM((C_HEADS, grp, tq), MXU_DTYPE),
        ],
        compiler_params=pltpu.CompilerParams(
            dimension_semantics=("parallel", "arbitrary"), vmem_limit_bytes=VMEM_LIMIT),
        name="dsa_attn",
    )(hb, hb, hb, hb, cvt, hb)


def _outproj_kernel(oa_ref, ob_ref, oc_ref, x_ref, w_ref, g_ref, b_ref, o_ref, *, alpha):
    half = oa_ref.shape[0] // 2
    for r in range(2):
        rows = slice(r * half, (r + 1) * half)
        y = _dot(oa_ref[rows, :], w_ref[0:A_WIDTH, :])
        y = y + _dot(ob_ref[rows, :], w_ref[A_WIDTH:A_WIDTH + B_WIDTH, :])
        y = y + _dot(oc_ref[rows, :], w_ref[A_WIDTH + B_WIDTH:, :])
        z = alpha * x_ref[rows, :] + y
        o_ref[rows, :] = _layer_norm_rows(z, g_ref[...], b_ref[...])


def _out_proj(oa, ob, oc, x, w, g, b, alpha):
    bsz, s, d = x.shape
    tm = min(ROW_TILE, s)
    row = lambda width: pl.BlockSpec((None, tm, width), lambda bb, i: (bb, i, 0))
    full = lambda a: pl.BlockSpec(a.shape, lambda bb, i: (0, 0))
    return pl.pallas_call(
        functools.partial(_outproj_kernel, alpha=alpha),
        grid=(bsz, s // tm),
        in_specs=[row(A_WIDTH), row(B_WIDTH), row(C_WIDTH), row(d), full(w), full(g), full(b)],
        out_specs=row(d),
        out_shape=jax.ShapeDtypeStruct((bsz, s, d), jnp.float32),
        compiler_params=pltpu.CompilerParams(
            dimension_semantics=("parallel", "parallel"), vmem_limit_bytes=VMEM_LIMIT),
        name="out_proj",
    )(oa, ob, oc, x, w, g, b)


def _ffn_kernel(x_ref, wg_ref, wu_ref, wd_ref, g_ref, b_ref, o_ref, acc_ref, *, alpha, fc):
    x = x_ref[...]
    xb = x.astype(MXU_DTYPE)
    hidden = wg_ref.shape[1]
    for c0 in range(0, hidden, fc):
        gate = _dot(xb, wg_ref[:, c0:c0 + fc])
        up = _dot(xb, wu_ref[:, c0:c0 + fc])
        hid = (jax.nn.silu(gate) * up).astype(MXU_DTYPE)
        part = _dot(hid, wd_ref[c0:c0 + fc, :])
        if c0 == 0:
            acc_ref[...] = part
        else:
            acc_ref[...] += part
    z = alpha * x + acc_ref[...]
    o_ref[...] = _layer_norm_rows(z, g_ref[...], b_ref[...])


def _ffn(x, wg, wu, wd, g, b, alpha):
    bsz, s, d = x.shape
    tm = min(ROW_TILE, s)
    row = pl.BlockSpec((None, tm, d), lambda bb, i: (bb, i, 0))
    full = lambda a: pl.BlockSpec(a.shape, lambda bb, i: (0, 0))
    return pl.pallas_call(
        functools.partial(_ffn_kernel, alpha=alpha, fc=256),
        grid=(bsz, s // tm),
        in_specs=[row, full(wg), full(wu), full(wd), full(g), full(b)],
        out_specs=row,
        out_shape=jax.ShapeDtypeStruct((bsz, s, d), jnp.float32),
        scratch_shapes=[pltpu.VMEM((tm, d), jnp.float32)],
        compiler_params=pltpu.CompilerParams(
            dimension_semantics=("parallel", "parallel"), vmem_limit_bytes=VMEM_LIMIT),
        name="ffn",
    )(x, wg, wu, wd, g, b)


def _prep_in_weights(w):
    d = w.shape[0]
    off_bq = 2 * A_WIDTH
    off_bk = off_bq + B_WIDTH
    off_bv = off_bk + B_WIDTH
    off_cq = off_bv + B_WIDTH
    off_ck = off_cq + C_WIDTH
    off_cv = off_ck + C_WIDTH
    off_iq = off_cv + C_WIDTH
    off_ik = off_iq + IDX_HEADS * IDX_DIM
    off_iw = off_ik + IDX_DIM
    q_scale = (B_QK_DIM ** -0.5) * LOG2E
    c_scale = (C_HEAD_DIM ** -0.5) * LOG2E
    i_scale = (IDX_HEADS ** -0.5) * (IDX_DIM ** -0.5)
    cols = [w[:, off_bq:off_bk] * q_scale, w[:, off_bk:off_bv], w[:, off_ck:off_cv],
            w[:, off_cq:off_ck] * c_scale, w[:, off_iq:off_iw],
            w[:, off_iw:off_iw + IDX_HEADS] * i_scale, jnp.zeros((d, LANES - IDX_DIM - IDX_HEADS), w.dtype)]
    wb = jnp.concatenate(cols, axis=1)
    wv = jnp.concatenate([w[:, off_bv:off_cq], w[:, off_cv:off_iq]], axis=1)
    return w[:, :off_bq].astype(MXU_DTYPE), wb.astype(MXU_DTYPE), wv.astype(MXU_DTYPE)


def kernel(x, w_in, gmlp_w_s, gmlp_b_s, gmlp_ln_g, gmlp_ln_b, lam_q1, lam_k1, lam_q2, lam_k2,
           diff_subln_g, w_out, ln1_g, ln1_b, w_gu, w_down, ln2_g, ln2_b):
    depth = w_in.shape[0]
    alpha = (2 * depth) ** 0.25
    hidden = w_down.shape[1]
    coefs_b = _alibi_coefs(B_HEADS)
    coefs_c = tuple(_alibi_coefs(C_HEADS))
    coef_tab = jnp.asarray([v for hi, lo, c in coefs_b for v in (hi, lo, c, 0.0)], jnp.float32)
    for l in range(depth):
        lam_init = 0.8 - 0.6 * math.exp(-0.3 * l)
        wa, wb, wv = _prep_in_weights(w_in[l])
        ha, hb, bvt, cvt = _in_proj(x, wa, wb, wv)
        out_a = _gmlp(ha, gmlp_w_s[l], jnp.repeat(gmlp_b_s[l].T, A_GDIM, axis=1),
                      gmlp_ln_g[l].reshape(1, A_WIDTH), gmlp_ln_b[l].reshape(1, A_WIDTH))
        lam_p = jnp.stack([lam_q1[l], lam_k1[l], lam_q2[l], lam_k2[l]]).astype(jnp.float32)
        out_b = _diff_attn(hb, bvt, coef_tab, lam_p, diff_subln_g[l].reshape(1, B_V_DIM), lam_init)
        out_c = _dsa_attn(hb, cvt, coefs_c)
        x = _out_proj(out_a, out_b, out_c, x, w_out[l].astype(MXU_DTYPE),
                      ln1_g[l].reshape(1, -1), ln1_b[l].reshape(1, -1), alpha)
        x = _ffn(x, w_gu[l, :, :hidden].astype(MXU_DTYPE), w_gu[l, :, hidden:].astype(MXU_DTYPE),
                 w_down[l].astype(MXU_DTYPE), ln2_g[l].reshape(1, -1), ln2_b[l].reshape(1, -1), alpha)
    return x
```

```python
import functools
import math
import struct

import jax
import jax.numpy as jnp
from jax import lax
from jax.experimental import pallas as pl
from jax.experimental.pallas import tpu as pltpu

CHUNK = 64
GMLP_BLOCK = 128
A_GROUPS, A_GDIM = 4, 64
A_WIDTH = A_GROUPS * A_GDIM
B_HEADS, B_QK_DIM = 4, 64
B_V_DIM = 2 * B_QK_DIM
B_WIDTH = B_HEADS * B_V_DIM
C_HEADS, C_HEAD_DIM = 4, 64
C_WIDTH = C_HEADS * C_HEAD_DIM
IDX_HEADS, IDX_DIM = 8, 32
TOPK_MAX = 256
LN_EPS = 1e-5
LOG2E = 1.4426950408889634

LANES = 128
SUBLANES = 8
WORD_BITS = 32
KEY_BLOCK = 256
V_PAD = 16

HB_BQ = 0
HB_BK = HB_BQ + B_WIDTH
HB_CK = HB_BK + 2 * B_WIDTH
HB_CQ = HB_CK + 2 * C_WIDTH
HB_IQ = HB_CQ + C_WIDTH
HB_IKW = HB_IQ + IDX_HEADS * IDX_DIM
HB_WIDTH = HB_IKW + LANES
IKW_W_OFF = IDX_DIM
C_FEAT_LANE = 0
WB_BQ = 0
WB_BK = WB_BQ + B_WIDTH
WB_CK = WB_BK + B_WIDTH
WB_REST = WB_CK + C_WIDTH

MXU_DTYPE = jnp.bfloat16
NEG_BIG = -1e30
B_VT_ROWS = B_V_DIM + V_PAD
C_VT_ROWS = C_HEAD_DIM + V_PAD
INT_MIN = -(2 ** 31)

ROW_TILE = 512
DSA_TQ = 256
DIFF_TQ = 2 * KEY_BLOCK
VMEM_LIMIT = 56 * 1024 * 1024


def _dot(a, b):
    return jnp.dot(a, b, preferred_element_type=jnp.float32)


def _layer_norm_rows(z, g, b):
    mu = jnp.mean(z, axis=-1, keepdims=True)
    zc = z - mu
    var = jnp.mean(zc * zc, axis=-1, keepdims=True)
    return zc * lax.rsqrt(var + LN_EPS) * g + b


def _bf16_round(v):
    bits = struct.unpack("<I", struct.pack("<f", v))[0]
    bits = ((bits + 0x7FFF + ((bits >> 16) & 1)) >> 16) << 16
    return struct.unpack("<f", struct.pack("<I", bits & 0xFFFFFFFF))[0]


def _alibi_coefs(n):
    out = []
    for h in range(n):
        c = (2.0 ** (-8.0 * (h + 1) / n)) * LOG2E
        hi = _bf16_round(c)
        out.append((hi, c - hi, c))
    return out


def _gmlp_tile(ha, ws_ref, bias_ref, g_ref, b_ref):
    f32 = jnp.float32
    nblk = ha.shape[0] // GMLP_BLOCK
    r = lax.broadcasted_iota(jnp.int32, (GMLP_BLOCK, GMLP_BLOCK), 0) // CHUNK
    c = lax.broadcasted_iota(jnp.int32, (GMLP_BLOCK, GMLP_BLOCK), 1) // CHUNK
    w_all = jnp.concatenate([jnp.where(c <= r, ws_ref[g], 0.0) for g in range(A_GROUPS)],
                            axis=0).astype(MXU_DTYPE)
    ar = lax.broadcasted_iota(jnp.int32, (A_WIDTH, A_WIDTH), 0) // A_GDIM
    ac = lax.broadcasted_iota(jnp.int32, (A_WIDTH, A_WIDTH), 1) // A_GDIM
    avg = jnp.where(ar == ac, 1.0 / A_GDIM, 0.0).astype(MXU_DTYPE)
    lane_grp = lax.broadcasted_iota(jnp.int32, (GMLP_BLOCK, A_WIDTH), 1) // A_GDIM

    def group_mean(z):
        hi = z.astype(MXU_DTYPE)
        lo = (z - hi.astype(f32)).astype(MXU_DTYPE)
        return _dot(hi, avg) + _dot(lo, avg)

    u = jax.nn.gelu(ha[:, 0:A_WIDTH])
    v = jax.nn.gelu(ha[:, A_WIDTH:2 * A_WIDTH])
    vc = v - group_mean(v)
    vn = (vc * lax.rsqrt(group_mean(vc * vc) + LN_EPS) * g_ref[...] + b_ref[...]).astype(MXU_DTYPE)
    outs = []
    for blk in range(nblk):
        rows = slice(blk * GMLP_BLOCK, (blk + 1) * GMLP_BLOCK)
        z = _dot(w_all, vn[rows, :])
        sg = bias_ref[...]
        for g in range(A_GROUPS):
            sg = sg + jnp.where(lane_grp == g, z[g * GMLP_BLOCK:(g + 1) * GMLP_BLOCK], 0.0)
        outs.append(u[rows, :] * sg)
    return jnp.concatenate(outs, axis=0)


def _inproj_kernel(x_ref, wa_ref, wb_ref, wv_ref, ws_ref, bias_ref, g_ref, b_ref,
                   oa_ref, hb_ref, bvt_ref, cvt_ref):
    tm = x_ref.shape[0]
    xb = x_ref[...].astype(MXU_DTYPE)
    oa_ref[...] = _gmlp_tile(_dot(xb, wa_ref[...]), ws_ref, bias_ref, g_ref, b_ref).astype(oa_ref.dtype)

    kloc = (lax.broadcasted_iota(jnp.int32, (tm, LANES), 0) % KEY_BLOCK).astype(jnp.float32)
    lane = lax.broadcasted_iota(jnp.int32, (tm, LANES), 1)
    low = lane < B_QK_DIM
    feat_k1 = jnp.where((lane == B_QK_DIM) | (lane == B_QK_DIM + 1), kloc, 0.0)
    feat_k2 = jnp.where(lane < 2, kloc, 0.0)
    feat_c = jnp.where(lane < 4, kloc, 0.0)

    hb_ref[:, HB_BQ:HB_BK] = _dot(xb, wb_ref[:, WB_BQ:WB_BK]).astype(hb_ref.dtype)
    yk = _dot(xb, wb_ref[:, WB_BK:WB_CK])
    blocks = []
    for h in range(B_HEADS):
        y = yk[:, h * LANES:(h + 1) * LANES]
        blocks += [jnp.where(low, y, feat_k1), jnp.where(low, feat_k2, y)]
    hb_ref[:, HB_BK:HB_CK] = jnp.concatenate(blocks, axis=1).astype(hb_ref.dtype)
    yc = _dot(xb, wb_ref[:, WB_CK:WB_REST])
    blocks = []
    for pr in range(C_HEADS // 2):
        blocks += [yc[:, pr * LANES:(pr + 1) * LANES], feat_c]
    hb_ref[:, HB_CK:HB_CQ] = jnp.concatenate(blocks, axis=1).astype(hb_ref.dtype)
    hb_ref[:, HB_CQ:HB_WIDTH] = _dot(xb, wb_ref[:, WB_REST:]).astype(hb_ref.dtype)

    yv = _dot(xb, wv_ref[...])
    ones_blk = jnp.where(lax.broadcasted_iota(jnp.int32, (V_PAD, KEY_BLOCK), 0) == 0, 1.0, 0.0)

    def heads_t(y, dim):
        y_t = y.T
        parts = []
        for h in range(y.shape[1] // dim):
            parts += [y_t[h * dim:(h + 1) * dim], ones_blk]
        return jnp.concatenate(parts, axis=0)

    for g in range(tm // KEY_BLOCK):
        rows = slice(g * KEY_BLOCK, (g + 1) * KEY_BLOCK)
        bvt_ref[g] = heads_t(yv[rows, 0:B_WIDTH], B_V_DIM).astype(bvt_ref.dtype)
        cvt_ref[g] = heads_t(yv[rows, B_WIDTH:], C_HEAD_DIM).astype(cvt_ref.dtype)


def _in_proj(x, wa, wb, wv, w_s, bias, ln_g, ln_b):
    bsz, s, d = x.shape
    tm = min(ROW_TILE, s)
    gpt = tm // KEY_BLOCK
    full = lambda a: pl.BlockSpec(a.shape, lambda b, i: (0,) * a.ndim)
    return pl.pallas_call(
        _inproj_kernel,
        grid=(bsz, s // tm),
        in_specs=[pl.BlockSpec((None, tm, d), lambda b, i: (b, i, 0)), full(wa), full(wb), full(wv),
                  full(w_s), full(bias), full(ln_g), full(ln_b)],
        out_specs=[
            pl.BlockSpec((None, tm, A_WIDTH), lambda b, i: (b, i, 0)),
            pl.BlockSpec((None, tm, HB_WIDTH), lambda b, i: (b, i, 0)),
            pl.BlockSpec((None, gpt, B_HEADS * B_VT_ROWS, KEY_BLOCK), lambda b, i: (b, i, 0, 0)),
            pl.BlockSpec((None, gpt, C_HEADS * C_VT_ROWS, KEY_BLOCK), lambda b, i: (b, i, 0, 0)),
        ],
        out_shape=[
            jax.ShapeDtypeStruct((bsz, s, A_WIDTH), MXU_DTYPE),
            jax.ShapeDtypeStruct((bsz, s, HB_WIDTH), MXU_DTYPE),
            jax.ShapeDtypeStruct((bsz, s // KEY_BLOCK, B_HEADS * B_VT_ROWS, KEY_BLOCK), MXU_DTYPE),
            jax.ShapeDtypeStruct((bsz, s // KEY_BLOCK, C_HEADS * C_VT_ROWS, KEY_BLOCK), MXU_DTYPE),
        ],
        compiler_params=pltpu.CompilerParams(
            dimension_semantics=("parallel", "parallel"), vmem_limit_bytes=VMEM_LIMIT),
        name="in_proj_gmlp",
    )(x, wa, wb, wv, w_s, bias, ln_g, ln_b)


def _softmax_step(s, coff, m):
    m_new = jnp.maximum(m, jnp.max(s, axis=0, keepdims=True) + coff)
    p = jnp.exp2(s - (m_new - coff)).astype(MXU_DTYPE)
    return m_new, jnp.exp2(m - m_new), p


def _diff_attn_kernel(coef_ref, q_ref, k1_ref, k2_ref, vt_ref, lam_ref, g_ref, o_ref,
                      w1_ref, w2_ref, acc_ref, sa_ref, sb_ref, pa_ref, pb_ref, *, lam_init):
    f32 = jnp.float32
    kb = KEY_BLOCK
    tq = DIFF_TQ
    h = pl.program_id(1)
    i = pl.program_id(2)
    c_hi, c_lo, c = coef_ref[4 * h], coef_ref[4 * h + 1], coef_ref[4 * h + 2]

    q_t = q_ref[...].astype(f32).T
    frow = lax.broadcasted_iota(jnp.int32, (LANES - B_QK_DIM, tq), 0)
    feat = jnp.where(frow == 0, c_hi, jnp.where(frow == 1, c_lo, 0.0))
    w1_ref[...] = jnp.concatenate([q_t[0:B_QK_DIM], feat], axis=0).astype(w1_ref.dtype)
    w2_ref[...] = jnp.concatenate([feat, q_t[B_QK_DIM:]], axis=0).astype(w2_ref.dtype)
    acc_ref[...] = jnp.zeros(acc_ref.shape, f32)
    pb_ref[...] = jnp.zeros(pb_ref.shape, pb_ref.dtype)
    qrow = (i * tq + lax.broadcasted_iota(jnp.int32, (1, tq), 1)).astype(f32)

    def issue_logits(j, dst_ref):
        start = pl.multiple_of(j * kb, kb)
        dst_ref[0] = _dot(k1_ref[pl.ds(start, kb), :], w1_ref[...])
        dst_ref[1] = _dot(k2_ref[pl.ds(start, kb), :], w2_ref[...])

    def issue_pv(j, p_ref):
        vt = vt_ref[j]
        return _dot(vt, p_ref[0]), _dot(vt, p_ref[1])

    def step(j, s_cur, s_nxt, p_cur, p_prev, fix, coff, state):
        m1, m2, a1_prev, a2_prev = state
        s1, s2 = s_cur[0], s_cur[1]
        if fix is not None:
            s1, s2 = s1 + fix, s2 + fix
        m1, a1, p1 = _softmax_step(s1, coff, m1)
        p_cur[0] = p1
        pv1, pv2 = issue_pv(jnp.maximum(j - 1, 0), p_prev)
        m2, a2, p2 = _softmax_step(s2, coff, m2)
        p_cur[1] = p2
        if s_nxt is not None:
            issue_logits(j + 1, s_nxt)
        acc_ref[0] = a1_prev * acc_ref[0] + pv1
        acc_ref[1] = a2_prev * acc_ref[1] + pv2
        return m1, m2, a1, a2

    def past_block(j, s_cur, s_nxt, p_cur, p_prev, state):
        coff = c * ((j * kb).astype(f32) - qrow)
        return step(j, s_cur, s_nxt, p_cur, p_prev, None, coff, state)

    def pair(t, state):
        state = past_block(2 * t, sa_ref, sb_ref, pa_ref, pb_ref, state)
        return past_block(2 * t + 1, sb_ref, sa_ref, pb_ref, pa_ref, state)

    m0 = jnp.full((1, tq), NEG_BIG, f32)
    one = jnp.ones((1, tq), f32)
    issue_logits(0, sa_ref)
    state = lax.fori_loop(0, i, pair, (m0, m0, one, one))

    kloc = lax.broadcasted_iota(jnp.int32, (kb, kb), 0)
    qloc = lax.broadcasted_iota(jnp.int32, (kb, kb), 1)
    diag = -c * (jnp.abs(qloc - kloc) + kloc).astype(f32)
    diag = jnp.where((kloc // CHUNK) <= (qloc // CHUNK), diag, NEG_BIG)
    zero_row = jnp.zeros((1, kb), f32)
    lo = 2 * i
    fix = jnp.concatenate([diag, jnp.zeros((kb, kb), f32)], axis=1)
    coff = jnp.concatenate([zero_row, c * ((lo * kb).astype(f32) - qrow[:, kb:])], axis=1)
    state = step(lo, sa_ref, sb_ref, pa_ref, pb_ref, fix, coff, state)
    fix = jnp.concatenate([jnp.full((kb, kb), NEG_BIG, f32), diag], axis=1)
    _, _, a1, a2 = step(lo + 1, sb_ref, None, pb_ref, pa_ref, fix, jnp.zeros((1, tq), f32), state)
    pv1, pv2 = issue_pv(lo + 1, pb_ref)
    n1 = a1 * acc_ref[0] + pv1
    n2 = a2 * acc_ref[1] + pv2
    o1 = n1[0:B_V_DIM] / n1[B_V_DIM:B_V_DIM + 1]
    o2 = n2[0:B_V_DIM] / n2[B_V_DIM:B_V_DIM + 1]

    lam_p = lam_ref[...]
    e1 = jnp.exp(jnp.sum(lam_p[0:1] * lam_p[1:2], axis=-1, keepdims=True))
    e2 = jnp.exp(jnp.sum(lam_p[2:3] * lam_p[3:4], axis=-1, keepdims=True))
    lam = e1 - e2 + lam_init
    o_t = o1 - lam * o2
    ms = jnp.mean(o_t * o_t, axis=0, keepdims=True)
    o = (o_t * lax.rsqrt(ms + LN_EPS)).T * g_ref[...] * (1.0 - lam_init)
    o_ref[...] = o.astype(o_ref.dtype)


def _diff_attn(hb, bvt, coefs, lam_p, subln_g, lam_init):
    bsz, s, _ = hb.shape
    kb, tq = KEY_BLOCK, DIFF_TQ
    qb, kcol = HB_BQ // LANES, HB_BK // LANES
    kern = functools.partial(_diff_attn_kernel, lam_init=lam_init)
    return pl.pallas_call(
        kern,
        grid=(bsz, B_HEADS, s // tq),
        in_specs=[
            pl.BlockSpec(memory_space=pltpu.SMEM),
            pl.BlockSpec((None, tq, LANES), lambda b, h, i: (b, i, qb + h)),
            pl.BlockSpec((None, s, LANES), lambda b, h, i: (b, 0, kcol + 2 * h)),
            pl.BlockSpec((None, s, LANES), lambda b, h, i: (b, 0, kcol + 2 * h + 1)),
            pl.BlockSpec((None, s // kb, B_VT_ROWS, kb), lambda b, h, i: (b, 0, h, 0)),
            pl.BlockSpec(lam_p.shape, lambda b, h, i: (0, 0)),
            pl.BlockSpec(subln_g.shape, lambda b, h, i: (0, 0)),
        ],
        out_specs=pl.BlockSpec((None, tq, LANES), lambda b, h, i: (b, i, h)),
        out_shape=jax.ShapeDtypeStruct((bsz, s, B_WIDTH), MXU_DTYPE),
        scratch_shapes=[
            pltpu.VMEM((LANES, tq), MXU_DTYPE),
            pltpu.VMEM((LANES, tq), MXU_DTYPE),
            pltpu.VMEM((2, B_VT_ROWS, tq), jnp.float32),
            pltpu.VMEM((2, kb, tq), jnp.float32),
            pltpu.VMEM((2, kb, tq), jnp.float32),
            pltpu.VMEM((2, kb, tq), MXU_DTYPE),
            pltpu.VMEM((2, kb, tq), MXU_DTYPE),
        ],
        compiler_params=pltpu.CompilerParams(
            dimension_semantics=("parallel", "parallel", "arbitrary"),
            vmem_limit_bytes=VMEM_LIMIT),
        name="diff_attn",
    )(coefs, hb, hb, hb, bvt, lam_p, subln_g)


def _dsa_kernel(cq_ref, iq_ref, wq_ref, ck_ref, cvt_ref, ik_ref, o_ref,
                key_ref, planes_ref, qt_ref, wpr_ref, acc_ref, sa_ref, sb_ref, pa_ref, pb_ref,
                *, topk, tq, coefs):
    f32 = jnp.float32
    grp = KEY_BLOCK
    i = pl.program_id(1)
    last = (i * tq) // grp
    n_grp = last + 1

    w_t = wq_ref[...].astype(f32).T[IKW_W_OFF:IKW_W_OFF + IDX_HEADS]
    iq_t = iq_ref[...].astype(f32).T
    zpad = jnp.zeros((LANES - IDX_DIM, tq), f32)
    for hh in range(IDX_HEADS):
        blk = jnp.concatenate([iq_t[hh * IDX_DIM:(hh + 1) * IDX_DIM], zpad], axis=0)
        qt_ref[:, hh * tq:(hh + 1) * tq] = blk.astype(qt_ref.dtype)
    cq_t = cq_ref[...].astype(f32).T
    zhead = jnp.zeros((C_HEAD_DIM, tq), f32)
    frow = lax.broadcasted_iota(jnp.int32, (LANES, tq), 0)
    for pr in range(C_HEADS // 2):
        halves = []
        for hb in range(2):
            hh = 2 * pr + hb
            c_hi, c_lo, _ = coefs[hh]
            qh = cq_t[hh * C_HEAD_DIM:(hh + 1) * C_HEAD_DIM]
            feat = jnp.where(frow == C_FEAT_LANE + 2 * hb, c_hi,
                             jnp.where(frow == C_FEAT_LANE + 2 * hb + 1, c_lo, 0.0))
            halves.append(jnp.concatenate([qh, zhead, feat] if hb == 0 else [zhead, qh, feat], axis=0))
        wpr_ref[pr] = jnp.concatenate(halves, axis=1).astype(wpr_ref.dtype)

    kloc = lax.broadcasted_iota(jnp.int32, (grp, tq), 0)
    qpos = i * tq + lax.broadcasted_iota(jnp.int32, (grp, tq), 1)
    qrow = i * tq + lax.broadcasted_iota(jnp.int32, (1, tq), 1)
    allowed_end = (qrow // CHUNK + 1) * CHUNK

    @pl.when(i == 0)
    def _():
        planes_ref[...] = jnp.zeros(planes_ref.shape, jnp.int32)

    def score_body(g, masked):
        start = pl.multiple_of(g * grp, grp)
        x = _dot(ik_ref[pl.ds(start, grp), :], qt_ref[...])
        sc = w_t[0:1] * jnp.maximum(x[:, 0:tq], 0.0)
        for hh in range(1, IDX_HEADS):
            sc = sc + w_t[hh:hh + 1] * jnp.maximum(x[:, hh * tq:(hh + 1) * tq], 0.0)
        bits = pltpu.bitcast(sc, jnp.int32)
        sign = bits >> 31
        key = ((bits & 0x7FFFFFFF) ^ sign) - sign
        if masked:
            key = jnp.where(g * grp + kloc < allowed_end, key, INT_MIN)
        key_ref[pl.ds(start, grp), :] = key
        w = [key[r * SUBLANES:(r + 1) * SUBLANES, :] for r in range(WORD_BITS)]
        j, msk = 16, 0x0000FFFF
        while j:
            k = 0
            while k < WORD_BITS:
                t = (w[k] ^ (w[k + j] >> j)) & msk
                w[k] = w[k] ^ t
                w[k + j] = w[k + j] ^ (t << j)
                k = (k + j + 1) & ~j
            j >>= 1
            msk = (msk ^ (msk << j)) & 0xFFFFFFFF
        w[0] = ~w[0]
        prow = pl.multiple_of(g * SUBLANES, SUBLANES)
        for r in range(WORD_BITS):
            planes_ref[r, pl.ds(prow, SUBLANES), :] = w[r]

    def score_pair(t, carry):
        score_body(2 * t, False)
        score_body(2 * t + 1, False)
        return carry

    lax.fori_loop(0, last // 2, score_pair, 0)

    @pl.when(last % 2 == 1)
    def _():
        score_body(last - 1, False)

    score_body(last, True)

    nrow = planes_ref.shape[1]
    prow_i = lax.broadcasted_iota(jnp.int32, (nrow, tq), 0)
    pbase = (prow_i >> 3) * grp + (prow_i & 7)

    def top_bits(n):
        n = jnp.clip(n, 0, WORD_BITS)
        return jnp.where(n > 0, lax.shift_left(jnp.int32(-1), (WORD_BITS - n) & (WORD_BITS - 1)), 0)

    def colsum(words):
        c = lax.population_count(words).reshape(nrow // SUBLANES, SUBLANES, tq)
        return jnp.sum(jnp.sum(c, axis=0), axis=0, keepdims=True)

    def bit_body(p, carry):
        cand, above, u = carry
        ones = cand & planes_ref[p]
        c1 = colsum(ones)
        take = (above + c1) >= topk
        cand = jnp.where(take, ones, cand ^ ones)
        above = jnp.where(take, above, above + c1)
        u = u | jnp.where(take, lax.shift_left(jnp.int32(1), WORD_BITS - 1 - p), 0)
        return cand, above, u

    cand0 = top_bits((allowed_end - (prow_i >> 3) * grp) >> 3)
    zero_row = jnp.zeros((1, tq), jnp.int32)
    ties, cnt_gt, u = lax.fori_loop(0, WORD_BITS, bit_body, (cand0, zero_row, zero_row))
    thr = u ^ INT_MIN
    room = topk - cnt_gt

    @pl.when(jnp.max(colsum(ties) - room) > 0)
    def _():
        nbits = max(1, (key_ref.shape[0]).bit_length())

        def tbody(b, c):
            cnd = c | lax.shift_left(jnp.int32(1), nbits - 1 - b)
            cnt = colsum(ties & top_bits((cnd - pbase + 7) >> 3))
            return jnp.where(cnt <= room, cnd, c)

        cut = lax.fori_loop(0, nbits, tbody, zero_row)

        def retire(g, carry):
            start = pl.multiple_of(g * grp, grp)
            keys = key_ref[pl.ds(start, grp), :]
            drop = (keys == thr) & (g * grp + kloc >= cut)
            key_ref[pl.ds(start, grp), :] = jnp.where(drop, INT_MIN, keys)
            return carry

        lax.fori_loop(0, n_grp, retire, 0)

    thr_open = jnp.where(thr > INT_MIN, thr - 1, thr)

    acc_ref[...] = jnp.zeros(acc_ref.shape, f32)
    pb_ref[...] = jnp.zeros(pb_ref.shape, pb_ref.dtype)
    qrow_f = qrow.astype(f32)

    def issue_logits(g, dst_ref):
        start = pl.multiple_of(g * grp, grp)
        for pr in range(C_HEADS // 2):
            dst_ref[pr] = _dot(ck_ref[pl.ds(start, grp), pr * 2 * LANES:(pr + 1) * 2 * LANES], wpr_ref[pr])

    def issue_pv(g, p_ref):
        vt = cvt_ref[g]
        return [_dot(vt[hh * C_VT_ROWS:(hh + 1) * C_VT_ROWS], p_ref[hh]) for hh in range(C_HEADS)]

    def step(g, s_cur, s_nxt, p_cur, p_prev, exact_bias, state):
        pvs = issue_pv(jnp.maximum(g - 1, 0), p_prev)
        start = pl.multiple_of(g * grp, grp)
        sel = key_ref[pl.ds(start, grp), :] > thr_open
        if exact_bias:
            dist = (jnp.abs(qpos - (g * grp + kloc)) + kloc).astype(f32)
        else:
            base = (g * grp).astype(f32) - qrow_f
        out = []
        for hh in range(C_HEADS):
            c = coefs[hh][2]
            s = s_cur[hh // 2, :, (hh % 2) * tq:(hh % 2 + 1) * tq]
            if exact_bias:
                s = s - c * dist
                coff = jnp.zeros((1, tq), f32)
            else:
                coff = c * base
            s = jnp.where(sel, s, NEG_BIG)
            m_new, alpha, p = _softmax_step(s, coff, state[2 * hh])
            p_cur[hh] = p
            out += [m_new, alpha]
        if s_nxt is not None:
            issue_logits(g + 1, s_nxt)
        for hh in range(C_HEADS):
            acc_ref[hh] = state[2 * hh + 1] * acc_ref[hh] + pvs[hh]
        return tuple(out)

    def last_group(s_cur, p_cur, p_prev, state):
        state = step(last, s_cur, None, p_cur, p_prev, True, state)
        pvs = issue_pv(last, p_cur)
        outs = []
        for hh in range(C_HEADS):
            n = state[2 * hh + 1] * acc_ref[hh] + pvs[hh]
            outs.append(n[0:C_HEAD_DIM] / n[C_HEAD_DIM:C_HEAD_DIM + 1])
        return jnp.concatenate(outs, axis=0)

    def pair(t, state):
        state = step(2 * t, sa_ref, sb_ref, pa_ref, pb_ref, False, state)
        return step(2 * t + 1, sb_ref, sa_ref, pb_ref, pa_ref, False, state)

    def odd_tail(state):
        state = step(last - 1, sa_ref, sb_ref, pa_ref, pb_ref, False, state)
        return last_group(sb_ref, pb_ref, pa_ref, state)

    m0 = jnp.full((1, tq), NEG_BIG, f32)
    one = jnp.ones((1, tq), f32)
    issue_logits(0, sa_ref)
    state = lax.fori_loop(0, last // 2, pair, (m0, one) * C_HEADS)
    o_t = lax.cond(last % 2 == 1, odd_tail, lambda st: last_group(sa_ref, pa_ref, pb_ref, st), state)
    o_ref[...] = o_t.T.astype(o_ref.dtype)


def _dsa_attn(hb, cvt, coefs):
    bsz, s, _ = hb.shape
    tq = min(DSA_TQ, s)
    grp = KEY_BLOCK
    topk = min(TOPK_MAX, s // 4)
    cqb, iqb = HB_CQ // C_WIDTH, HB_IQ // C_WIDTH
    ckb, ikwb = HB_CK // (2 * C_WIDTH), HB_IKW // LANES
    kern = functools.partial(_dsa_kernel, topk=topk, tq=tq, coefs=coefs)
    return pl.pallas_call(
        kern,
        grid=(bsz, s // tq),
        in_specs=[
            pl.BlockSpec((None, tq, C_WIDTH), lambda b, i: (b, i, cqb)),
            pl.BlockSpec((None, tq, C_WIDTH), lambda b, i: (b, i, iqb)),
            pl.BlockSpec((None, tq, LANES), lambda b, i: (b, i, ikwb)),
            pl.BlockSpec((None, s, 2 * C_WIDTH), lambda b, i: (b, 0, ckb)),
            pl.BlockSpec((None, s // grp, C_HEADS * C_VT_ROWS, grp), lambda b, i: (b, 0, 0, 0)),
            pl.BlockSpec((None, s, LANES), lambda b, i: (b, 0, ikwb)),
        ],
        out_specs=pl.BlockSpec((None, tq, C_WIDTH), lambda b, i: (b, i, 0)),
        out_shape=jax.ShapeDtypeStruct((bsz, s, C_WIDTH), MXU_DTYPE),
        scratch_shapes=[
            pltpu.VMEM((s, tq), jnp.int32),
            pltpu.VMEM((WORD_BITS, (s // grp) * SUBLANES, tq), jnp.int32),
            pltpu.VMEM((LANES, IDX_HEADS * tq), MXU_DTYPE),
            pltpu.VMEM((C_HEADS // 2, 2 * LANES, 2 * tq), MXU_DTYPE),
            pltpu.VMEM((C_HEADS, C_VT_ROWS, tq), jnp.float32),
            pltpu.VMEM((C_HEADS // 2, grp, 2 * tq), jnp.float32),
            pltpu.VMEM((C_HEADS // 2, grp, 2 * tq), jnp.float32),
            pltpu.VMEM((C_HEADS, grp, tq), MXU_DTYPE),
            pltpu.VMEM((C_HEADS, grp, tq), MXU_DTYPE),
        ],
        compiler_params=pltpu.CompilerParams(
            dimension_semantics=("parallel", "arbitrary"), vmem_limit_bytes=VMEM_LIMIT),
        name="dsa_attn",
    )(hb, hb, hb, hb, cvt, hb)


def _mix_ffn_kernel(oa_ref, ob_ref, oc_ref, x_ref, wo_ref, g1_ref, b1_ref, wg_ref, wu_ref, wd_ref,
                    g2_ref, b2_ref, o_ref, x1_ref, acc_ref, *, alpha, fc):
    half = oa_ref.shape[0] // 2
    for r in range(2):
        rows = slice(r * half, (r + 1) * half)
        y = _dot(oa_ref[rows, :], wo_ref[0:A_WIDTH, :])
        y = y + _dot(ob_ref[rows, :], wo_ref[A_WIDTH:A_WIDTH + B_WIDTH, :])
        y = y + _dot(oc_ref[rows, :], wo_ref[A_WIDTH + B_WIDTH:, :])
        z = alpha * x_ref[rows, :] + y
        x1_ref[rows, :] = _layer_norm_rows(z, g1_ref[...], b1_ref[...])
    xb = x1_ref[...].astype(MXU_DTYPE)
    hidden = wg_ref.shape[1]
    for c0 in range(0, hidden, fc):
        gate = _dot(xb, wg_ref[:, c0:c0 + fc])
        up = _dot(xb, wu_ref[:, c0:c0 + fc])
        hid = (jax.nn.silu(gate) * up).astype(MXU_DTYPE)
        part = _dot(hid, wd_ref[c0:c0 + fc, :])
        if c0 == 0:
            acc_ref[...] = part
        else:
            acc_ref[...] += part
    z = alpha * x1_ref[...] + acc_ref[...]
    o_ref[...] = _layer_norm_rows(z, g2_ref[...], b2_ref[...])


def _mix_ffn(oa, ob, oc, x, wo, g1, b1, wg, wu, wd, g2, b2, alpha):
    bsz, s, d = x.shape
    tm = min(ROW_TILE, s)
    row = lambda width: pl.BlockSpec((None, tm, width), lambda bb, i: (bb, i, 0))
    full = lambda a: pl.BlockSpec(a.shape, lambda bb, i: (0, 0), pipeline_mode=pl.Buffered(1))
    return pl.pallas_call(
        functools.partial(_mix_ffn_kernel, alpha=alpha, fc=256),
        grid=(bsz, s // tm),
        in_specs=[row(A_WIDTH), row(B_WIDTH), row(C_WIDTH), row(d), full(wo), full(g1), full(b1),
                  full(wg), full(wu), full(wd), full(g2), full(b2)],
        out_specs=row(d),
        out_shape=jax.ShapeDtypeStruct((bsz, s, d), jnp.float32),
        scratch_shapes=[pltpu.VMEM((tm, d), jnp.float32), pltpu.VMEM((tm, d), jnp.float32)],
        compiler_params=pltpu.CompilerParams(
            dimension_semantics=("parallel", "parallel"), vmem_limit_bytes=VMEM_LIMIT),
        name="out_proj_ffn",
    )(oa, ob, oc, x, wo, g1, b1, wg, wu, wd, g2, b2)


def _prep_in_weights(w):
    d = w.shape[0]
    off_bq = 2 * A_WIDTH
    off_bk = off_bq + B_WIDTH
    off_bv = off_bk + B_WIDTH
    off_cq = off_bv + B_WIDTH
    off_ck = off_cq + C_WIDTH
    off_cv = off_ck + C_WIDTH
    off_iq = off_cv + C_WIDTH
    off_ik = off_iq + IDX_HEADS * IDX_DIM
    off_iw = off_ik + IDX_DIM
    q_scale = (B_QK_DIM ** -0.5) * LOG2E
    c_scale = (C_HEAD_DIM ** -0.5) * LOG2E
    i_scale = (IDX_HEADS ** -0.5) * (IDX_DIM ** -0.5)
    cols = [w[:, off_bq:off_bk] * q_scale, w[:, off_bk:off_bv], w[:, off_ck:off_cv],
            w[:, off_cq:off_ck] * c_scale, w[:, off_iq:off_iw],
            w[:, off_iw:off_iw + IDX_HEADS] * i_scale, jnp.zeros((d, LANES - IDX_DIM - IDX_HEADS), w.dtype)]
    wb = jnp.concatenate(cols, axis=1)
    wv = jnp.concatenate([w[:, off_bv:off_cq], w[:, off_cv:off_iq]], axis=1)
    return w[:, :off_bq].astype(MXU_DTYPE), wb.astype(MXU_DTYPE), wv.astype(MXU_DTYPE)


def kernel(x, w_in, gmlp_w_s, gmlp_b_s, gmlp_ln_g, gmlp_ln_b, lam_q1, lam_k1, lam_q2, lam_k2,
           diff_subln_g, w_out, ln1_g, ln1_b, w_gu, w_down, ln2_g, ln2_b):
    depth = w_in.shape[0]
    alpha = (2 * depth) ** 0.25
    hidden = w_down.shape[1]
    coefs_b = _alibi_coefs(B_HEADS)
    coefs_c = tuple(_alibi_coefs(C_HEADS))
    coef_tab = jnp.asarray([v for hi, lo, c in coefs_b for v in (hi, lo, c, 0.0)], jnp.float32)
    for l in range(depth):
        lam_init = 0.8 - 0.6 * math.exp(-0.3 * l)
        wa, wb, wv = _prep_in_weights(w_in[l])
        out_a, hb, bvt, cvt = _in_proj(x, wa, wb, wv, gmlp_w_s[l], jnp.repeat(gmlp_b_s[l].T, A_GDIM, axis=1),
                                       gmlp_ln_g[l].reshape(1, A_WIDTH), gmlp_ln_b[l].reshape(1, A_WIDTH))
        lam_p = jnp.stack([lam_q1[l], lam_k1[l], lam_q2[l], lam_k2[l]]).astype(jnp.float32)
        out_b = _diff_attn(hb, bvt, coef_tab, lam_p, diff_subln_g[l].reshape(1, B_V_DIM), lam_init)
        out_c = _dsa_attn(hb, cvt, coefs_c)
        x = _mix_ffn(out_a, out_b, out_c, x, w_out[l].astype(MXU_DTYPE),
                     ln1_g[l].reshape(1, -1), ln1_b[l].reshape(1, -1),
                     w_gu[l, :, :hidden].astype(MXU_DTYPE), w_gu[l, :, hidden:].astype(MXU_DTYPE),
                     w_down[l].astype(MXU_DTYPE), ln2_g[l].reshape(1, -1), ln2_b[l].reshape(1, -1), alpha)
    return x
```

```python
import functools
import math
import struct

import jax
import jax.numpy as jnp
from jax import lax
from jax.experimental import pallas as pl
from jax.experimental.pallas import tpu as pltpu

CHUNK = 64
GMLP_BLOCK = 128
A_GROUPS, A_GDIM = 4, 64
A_WIDTH = A_GROUPS * A_GDIM
B_HEADS, B_QK_DIM = 4, 64
B_V_DIM = 2 * B_QK_DIM
B_WIDTH = B_HEADS * B_V_DIM
C_HEADS, C_HEAD_DIM = 4, 64
C_WIDTH = C_HEADS * C_HEAD_DIM
IDX_HEADS, IDX_DIM = 8, 32
TOPK_MAX = 256
LN_EPS = 1e-5
LOG2E = 1.4426950408889634

LANES = 128
SUBLANES = 8
WORD_BITS = 32
KEY_BLOCK = 256
V_PAD = 16

HB_BQ = 0
HB_BK = HB_BQ + B_WIDTH
HB_CK = HB_BK + 2 * B_WIDTH
HB_CQ = HB_CK + 2 * C_WIDTH
HB_IQ = HB_CQ + C_WIDTH
HB_IKW = HB_IQ + IDX_HEADS * IDX_DIM
HB_WIDTH = HB_IKW + LANES
IKW_W_OFF = IDX_DIM
C_FEAT_LANE = 0
WB_BQ = 0
WB_BK = WB_BQ + B_WIDTH
WB_CK = WB_BK + B_WIDTH
WB_REST = WB_CK + C_WIDTH

MXU_DTYPE = jnp.bfloat16
NEG_BIG = -1e30
B_VT_ROWS = B_V_DIM + V_PAD
C_VT_ROWS = C_HEAD_DIM + V_PAD
INT_MIN = -(2 ** 31)

ROW_TILE = 512
DSA_TQ = 256
DIFF_TQ = 2 * KEY_BLOCK
VMEM_LIMIT = 56 * 1024 * 1024


def _dot(a, b):
    return jnp.dot(a, b, preferred_element_type=jnp.float32)


def _layer_norm_rows(z, g, b):
    mu = jnp.mean(z, axis=-1, keepdims=True)
    zc = z - mu
    var = jnp.mean(zc * zc, axis=-1, keepdims=True)
    return zc * lax.rsqrt(var + LN_EPS) * g + b


def _bf16_round(v):
    bits = struct.unpack("<I", struct.pack("<f", v))[0]
    bits = ((bits + 0x7FFF + ((bits >> 16) & 1)) >> 16) << 16
    return struct.unpack("<f", struct.pack("<I", bits & 0xFFFFFFFF))[0]


def _alibi_coefs(n):
    out = []
    for h in range(n):
        c = (2.0 ** (-8.0 * (h + 1) / n)) * LOG2E
        hi = _bf16_round(c)
        out.append((hi, c - hi, c))
    return out


def _gmlp_tile(ha, ws_ref, bias_ref, g_ref, b_ref):
    f32 = jnp.float32
    nblk = ha.shape[0] // GMLP_BLOCK
    r = lax.broadcasted_iota(jnp.int32, (GMLP_BLOCK, GMLP_BLOCK), 0) // CHUNK
    c = lax.broadcasted_iota(jnp.int32, (GMLP_BLOCK, GMLP_BLOCK), 1) // CHUNK
    w_all = jnp.concatenate([jnp.where(c <= r, ws_ref[g], 0.0) for g in range(A_GROUPS)],
                            axis=0).astype(MXU_DTYPE)
    ar = lax.broadcasted_iota(jnp.int32, (A_WIDTH, A_WIDTH), 0) // A_GDIM
    ac = lax.broadcasted_iota(jnp.int32, (A_WIDTH, A_WIDTH), 1) // A_GDIM
    avg = jnp.where(ar == ac, 1.0 / A_GDIM, 0.0).astype(MXU_DTYPE)
    lane_grp = lax.broadcasted_iota(jnp.int32, (GMLP_BLOCK, A_WIDTH), 1) // A_GDIM

    def group_mean(z):
        hi = z.astype(MXU_DTYPE)
        lo = (z - hi.astype(f32)).astype(MXU_DTYPE)
        return _dot(hi, avg) + _dot(lo, avg)

    u = jax.nn.gelu(ha[:, 0:A_WIDTH])
    v = jax.nn.gelu(ha[:, A_WIDTH:2 * A_WIDTH])
    vc = v - group_mean(v)
    vn = (vc * lax.rsqrt(group_mean(vc * vc) + LN_EPS) * g_ref[...] + b_ref[...]).astype(MXU_DTYPE)
    outs = []
    for blk in range(nblk):
        rows = slice(blk * GMLP_BLOCK, (blk + 1) * GMLP_BLOCK)
        z = _dot(w_all, vn[rows, :])
        sg = bias_ref[...]
        for g in range(A_GROUPS):
            sg = sg + jnp.where(lane_grp == g, z[g * GMLP_BLOCK:(g + 1) * GMLP_BLOCK], 0.0)
        outs.append(u[rows, :] * sg)
    return jnp.concatenate(outs, axis=0)


def _inproj_kernel(x_ref, wa_ref, wb_ref, wv_ref, ws_ref, bias_ref, g_ref, b_ref,
                   oa_ref, hb_ref, bvt_ref, cvt_ref):
    tm = x_ref.shape[0]
    xb = x_ref[...].astype(MXU_DTYPE)
    oa_ref[...] = _gmlp_tile(_dot(xb, wa_ref[...]), ws_ref, bias_ref, g_ref, b_ref).astype(oa_ref.dtype)

    kloc = (lax.broadcasted_iota(jnp.int32, (tm, LANES), 0) % KEY_BLOCK).astype(jnp.float32)
    lane = lax.broadcasted_iota(jnp.int32, (tm, LANES), 1)
    low = lane < B_QK_DIM
    feat_k1 = jnp.where((lane == B_QK_DIM) | (lane == B_QK_DIM + 1), kloc, 0.0)
    feat_k2 = jnp.where(lane < 2, kloc, 0.0)
    feat_c = jnp.where(lane < 4, kloc, 0.0)

    hb_ref[:, HB_BQ:HB_BK] = _dot(xb, wb_ref[:, WB_BQ:WB_BK]).astype(hb_ref.dtype)
    yk = _dot(xb, wb_ref[:, WB_BK:WB_CK])
    blocks = []
    for h in range(B_HEADS):
        y = yk[:, h * LANES:(h + 1) * LANES]
        blocks += [jnp.where(low, y, feat_k1), jnp.where(low, feat_k2, y)]
    hb_ref[:, HB_BK:HB_CK] = jnp.concatenate(blocks, axis=1).astype(hb_ref.dtype)
    yc = _dot(xb, wb_ref[:, WB_CK:WB_REST])
    blocks = []
    for pr in range(C_HEADS // 2):
        blocks += [yc[:, pr * LANES:(pr + 1) * LANES], feat_c]
    hb_ref[:, HB_CK:HB_CQ] = jnp.concatenate(blocks, axis=1).astype(hb_ref.dtype)
    hb_ref[:, HB_CQ:HB_WIDTH] = _dot(xb, wb_ref[:, WB_REST:]).astype(hb_ref.dtype)

    yv = _dot(xb, wv_ref[...])
    ones_blk = jnp.where(lax.broadcasted_iota(jnp.int32, (V_PAD, KEY_BLOCK), 0) == 0, 1.0, 0.0)

    def heads_t(y, dim):
        y_t = y.T
        parts = []
        for h in range(y.shape[1] // dim):
            parts += [y_t[h * dim:(h + 1) * dim], ones_blk]
        return jnp.concatenate(parts, axis=0)

    for g in range(tm // KEY_BLOCK):
        rows = slice(g * KEY_BLOCK, (g + 1) * KEY_BLOCK)
        bvt_ref[g] = heads_t(yv[rows, 0:B_WIDTH], B_V_DIM).astype(bvt_ref.dtype)
        cvt_ref[g] = heads_t(yv[rows, B_WIDTH:], C_HEAD_DIM).astype(cvt_ref.dtype)


def _in_proj(x, wa, wb, wv, w_s, bias, ln_g, ln_b):
    bsz, s, d = x.shape
    tm = min(ROW_TILE, s)
    gpt = tm // KEY_BLOCK
    full = lambda a: pl.BlockSpec(a.shape, lambda b, i: (0,) * a.ndim)
    return pl.pallas_call(
        _inproj_kernel,
        grid=(bsz, s // tm),
        in_specs=[pl.BlockSpec((None, tm, d), lambda b, i: (b, i, 0)), full(wa), full(wb), full(wv),
                  full(w_s), full(bias), full(ln_g), full(ln_b)],
        out_specs=[
            pl.BlockSpec((None, tm, A_WIDTH), lambda b, i: (b, i, 0)),
            pl.BlockSpec((None, tm, HB_WIDTH), lambda b, i: (b, i, 0)),
            pl.BlockSpec((None, gpt, B_HEADS * B_VT_ROWS, KEY_BLOCK), lambda b, i: (b, i, 0, 0)),
            pl.BlockSpec((None, gpt, C_HEADS * C_VT_ROWS, KEY_BLOCK), lambda b, i: (b, i, 0, 0)),
        ],
        out_shape=[
            jax.ShapeDtypeStruct((bsz, s, A_WIDTH), MXU_DTYPE),
            jax.ShapeDtypeStruct((bsz, s, HB_WIDTH), MXU_DTYPE),
            jax.ShapeDtypeStruct((bsz, s // KEY_BLOCK, B_HEADS * B_VT_ROWS, KEY_BLOCK), MXU_DTYPE),
            jax.ShapeDtypeStruct((bsz, s // KEY_BLOCK, C_HEADS * C_VT_ROWS, KEY_BLOCK), MXU_DTYPE),
        ],
        compiler_params=pltpu.CompilerParams(
            dimension_semantics=("parallel", "parallel"), vmem_limit_bytes=VMEM_LIMIT),
        name="in_proj_gmlp",
    )(x, wa, wb, wv, w_s, bias, ln_g, ln_b)


def _softmax_step(s, coff, m):
    m_new = jnp.maximum(m, jnp.max(s, axis=0, keepdims=True) + coff)
    p = jnp.exp2(s - (m_new - coff)).astype(MXU_DTYPE)
    return m_new, jnp.exp2(m - m_new), p


def _diff_attn_kernel(coef_ref, q_ref, k1_ref, k2_ref, vt_ref, lam_ref, g_ref, o_ref,
                      w1_ref, w2_ref, acc_ref, sa_ref, sb_ref, pa_ref, pb_ref, *, lam_init):
    f32 = jnp.float32
    kb = KEY_BLOCK
    tq = DIFF_TQ
    h = pl.program_id(1)
    i = pl.program_id(2)
    c_hi, c_lo, c = coef_ref[4 * h], coef_ref[4 * h + 1], coef_ref[4 * h + 2]

    q_t = q_ref[...].astype(f32).T
    frow = lax.broadcasted_iota(jnp.int32, (LANES - B_QK_DIM, tq), 0)
    feat = jnp.where(frow == 0, c_hi, jnp.where(frow == 1, c_lo, 0.0))
    w1_ref[...] = jnp.concatenate([q_t[0:B_QK_DIM], feat], axis=0).astype(w1_ref.dtype)
    w2_ref[...] = jnp.concatenate([feat, q_t[B_QK_DIM:]], axis=0).astype(w2_ref.dtype)
    acc_ref[...] = jnp.zeros(acc_ref.shape, f32)
    pb_ref[...] = jnp.zeros(pb_ref.shape, pb_ref.dtype)
    qrow = (i * tq + lax.broadcasted_iota(jnp.int32, (1, tq), 1)).astype(f32)

    def issue_logits(j, dst_ref, cols=slice(None)):
        start = pl.multiple_of(j * kb, kb)
        dst_ref[0, :, cols] = _dot(k1_ref[pl.ds(start, kb), :], w1_ref[:, cols])
        dst_ref[1, :, cols] = _dot(k2_ref[pl.ds(start, kb), :], w2_ref[:, cols])

    def issue_pv(j, p_ref):
        vt = vt_ref[j]
        return _dot(vt, p_ref[0]), _dot(vt, p_ref[1])

    def step(j, s_cur, s_nxt, p_cur, p_prev, fix, coff, state, nxt_cols=slice(None)):
        m1, m2, a1_prev, a2_prev = state
        s1, s2 = s_cur[0], s_cur[1]
        if fix is not None:
            s1, s2 = s1 + fix, s2 + fix
        m1, a1, p1 = _softmax_step(s1, coff, m1)
        p_cur[0] = p1
        pv1, pv2 = issue_pv(jnp.maximum(j - 1, 0), p_prev)
        m2, a2, p2 = _softmax_step(s2, coff, m2)
        p_cur[1] = p2
        if s_nxt is not None:
            issue_logits(j + 1, s_nxt, nxt_cols)
        acc_ref[0] = a1_prev * acc_ref[0] + pv1
        acc_ref[1] = a2_prev * acc_ref[1] + pv2
        return m1, m2, a1, a2

    def past_block(j, s_cur, s_nxt, p_cur, p_prev, state):
        coff = c * ((j * kb).astype(f32) - qrow)
        return step(j, s_cur, s_nxt, p_cur, p_prev, None, coff, state)

    def pair(t, state):
        state = past_block(2 * t, sa_ref, sb_ref, pa_ref, pb_ref, state)
        return past_block(2 * t + 1, sb_ref, sa_ref, pb_ref, pa_ref, state)

    m0 = jnp.full((1, tq), NEG_BIG, f32)
    one = jnp.ones((1, tq), f32)
    issue_logits(0, sa_ref)
    state = lax.fori_loop(0, i, pair, (m0, m0, one, one))

    kloc = lax.broadcasted_iota(jnp.int32, (kb, kb), 0)
    qloc = lax.broadcasted_iota(jnp.int32, (kb, kb), 1)
    diag = -c * (jnp.abs(qloc - kloc) + kloc).astype(f32)
    diag = jnp.where((kloc // CHUNK) <= (qloc // CHUNK), diag, NEG_BIG)
    zero_row = jnp.zeros((1, kb), f32)
    lo = 2 * i
    late = slice(kb, tq)
    fix = jnp.concatenate([diag, jnp.zeros((kb, kb), f32)], axis=1)
    coff = jnp.concatenate([zero_row, c * ((lo * kb).astype(f32) - qrow[:, late])], axis=1)
    m1, m2, a1, a2 = step(lo, sa_ref, sb_ref, pa_ref, pb_ref, fix, coff, state, nxt_cols=late)
    pv1, pv2 = issue_pv(lo, pa_ref)
    m1h, a1h, p1 = _softmax_step(sb_ref[0, :, late] + diag, zero_row, m1[:, late])
    m2h, a2h, p2 = _softmax_step(sb_ref[1, :, late] + diag, zero_row, m2[:, late])
    vt = vt_ref[lo + 1]
    n1 = a1 * acc_ref[0] + pv1
    n2 = a2 * acc_ref[1] + pv2
    n1 = jnp.concatenate([n1[:, :kb], a1h * n1[:, late] + _dot(vt, p1)], axis=1)
    n2 = jnp.concatenate([n2[:, :kb], a2h * n2[:, late] + _dot(vt, p2)], axis=1)
    o1 = n1[0:B_V_DIM] / n1[B_V_DIM:B_V_DIM + 1]
    o2 = n2[0:B_V_DIM] / n2[B_V_DIM:B_V_DIM + 1]

    lam_p = lam_ref[...]
    e1 = jnp.exp(jnp.sum(lam_p[0:1] * lam_p[1:2], axis=-1, keepdims=True))
    e2 = jnp.exp(jnp.sum(lam_p[2:3] * lam_p[3:4], axis=-1, keepdims=True))
    lam = e1 - e2 + lam_init
    o_t = o1 - lam * o2
    ms = jnp.mean(o_t * o_t, axis=0, keepdims=True)
    o = (o_t * lax.rsqrt(ms + LN_EPS)).T * g_ref[...] * (1.0 - lam_init)
    o_ref[...] = o.astype(o_ref.dtype)


def _diff_attn(hb, bvt, coefs, lam_p, subln_g, lam_init):
    bsz, s, _ = hb.shape
    kb, tq = KEY_BLOCK, DIFF_TQ
    qb, kcol = HB_BQ // LANES, HB_BK // LANES
    kern = functools.partial(_diff_attn_kernel, lam_init=lam_init)
    return pl.pallas_call(
        kern,
        grid=(bsz, B_HEADS, s // tq),
        in_specs=[
            pl.BlockSpec(memory_space=pltpu.SMEM),
            pl.BlockSpec((None, tq, LANES), lambda b, h, i: (b, i, qb + h)),
            pl.BlockSpec((None, s, LANES), lambda b, h, i: (b, 0, kcol + 2 * h)),
            pl.BlockSpec((None, s, LANES), lambda b, h, i: (b, 0, kcol + 2 * h + 1)),
            pl.BlockSpec((None, s // kb, B_VT_ROWS, kb), lambda b, h, i: (b, 0, h, 0)),
            pl.BlockSpec(lam_p.shape, lambda b, h, i: (0, 0)),
            pl.BlockSpec(subln_g.shape, lambda b, h, i: (0, 0)),
        ],
        out_specs=pl.BlockSpec((None, tq, LANES), lambda b, h, i: (b, i, h)),
        out_shape=jax.ShapeDtypeStruct((bsz, s, B_WIDTH), MXU_DTYPE),
        scratch_shapes=[
            pltpu.VMEM((LANES, tq), MXU_DTYPE),
            pltpu.VMEM((LANES, tq), MXU_DTYPE),
            pltpu.VMEM((2, B_VT_ROWS, tq), jnp.float32),
            pltpu.VMEM((2, kb, tq), jnp.float32),
            pltpu.VMEM((2, kb, tq), jnp.float32),
            pltpu.VMEM((2, kb, tq), MXU_DTYPE),
            pltpu.VMEM((2, kb, tq), MXU_DTYPE),
        ],
        compiler_params=pltpu.CompilerParams(
            dimension_semantics=("parallel", "parallel", "arbitrary"),
            vmem_limit_bytes=VMEM_LIMIT),
        name="diff_attn",
    )(coefs, hb, hb, hb, bvt, lam_p, subln_g)


def _dsa_kernel(cq_ref, iq_ref, wq_ref, ck_ref, cvt_ref, ik_ref, o_ref,
                key_ref, planes_ref, qt_ref, wpr_ref, acc_ref, sa_ref, sb_ref, pa_ref, pb_ref,
                *, topk, tq, coefs):
    f32 = jnp.float32
    grp = KEY_BLOCK
    i = pl.program_id(1)
    last = (i * tq) // grp
    n_grp = last + 1

    w_t = wq_ref[...].astype(f32).T[IKW_W_OFF:IKW_W_OFF + IDX_HEADS]
    iq_t = iq_ref[...].astype(f32).T
    zpad = jnp.zeros((LANES - IDX_DIM, tq), f32)
    for hh in range(IDX_HEADS):
        blk = jnp.concatenate([iq_t[hh * IDX_DIM:(hh + 1) * IDX_DIM], zpad], axis=0)
        qt_ref[:, hh * tq:(hh + 1) * tq] = blk.astype(qt_ref.dtype)
    cq_t = cq_ref[...].astype(f32).T
    zhead = jnp.zeros((C_HEAD_DIM, tq), f32)
    frow = lax.broadcasted_iota(jnp.int32, (LANES, tq), 0)
    for pr in range(C_HEADS // 2):
        halves = []
        for hb in range(2):
            hh = 2 * pr + hb
            c_hi, c_lo, _ = coefs[hh]
            qh = cq_t[hh * C_HEAD_DIM:(hh + 1) * C_HEAD_DIM]
            feat = jnp.where(frow == C_FEAT_LANE + 2 * hb, c_hi,
                             jnp.where(frow == C_FEAT_LANE + 2 * hb + 1, c_lo, 0.0))
            halves.append(jnp.concatenate([qh, zhead, feat] if hb == 0 else [zhead, qh, feat], axis=0))
        wpr_ref[pr] = jnp.concatenate(halves, axis=1).astype(wpr_ref.dtype)

    kloc = lax.broadcasted_iota(jnp.int32, (grp, tq), 0)
    qpos = i * tq + lax.broadcasted_iota(jnp.int32, (grp, tq), 1)
    qrow = i * tq + lax.broadcasted_iota(jnp.int32, (1, tq), 1)
    allowed_end = (qrow // CHUNK + 1) * CHUNK

    @pl.when(i == 0)
    def _():
        planes_ref[...] = jnp.zeros(planes_ref.shape, jnp.int32)

    def score_body(g, masked):
        start = pl.multiple_of(g * grp, grp)
        x = _dot(ik_ref[pl.ds(start, grp), :], qt_ref[...])
        sc = w_t[0:1] * jnp.maximum(x[:, 0:tq], 0.0)
        for hh in range(1, IDX_HEADS):
            sc = sc + w_t[hh:hh + 1] * jnp.maximum(x[:, hh * tq:(hh + 1) * tq], 0.0)
        bits = pltpu.bitcast(sc, jnp.int32)
        sign = bits >> 31
        key = ((bits & 0x7FFFFFFF) ^ sign) - sign
        if masked:
            key = jnp.where(g * grp + kloc < allowed_end, key, INT_MIN)
        key_ref[pl.ds(start, grp), :] = key
        w = [key[r * SUBLANES:(r + 1) * SUBLANES, :] for r in range(WORD_BITS)]
        j, msk = 16, 0x0000FFFF
        while j:
            k = 0
            while k < WORD_BITS:
                t = (w[k] ^ (w[k + j] >> j)) & msk
                w[k] = w[k] ^ t
                w[k + j] = w[k + j] ^ (t << j)
                k = (k + j + 1) & ~j
            j >>= 1
            msk = (msk ^ (msk << j)) & 0xFFFFFFFF
        w[0] = ~w[0]
        prow = pl.multiple_of(g * SUBLANES, SUBLANES)
        for r in range(WORD_BITS):
            planes_ref[r, pl.ds(prow, SUBLANES), :] = w[r]

    def score_pair(t, carry):
        score_body(2 * t, False)
        score_body(2 * t + 1, False)
        return carry

    lax.fori_loop(0, last // 2, score_pair, 0)

    @pl.when(last % 2 == 1)
    def _():
        score_body(last - 1, False)

    score_body(last, True)

    nrow = planes_ref.shape[1]
    prow_i = lax.broadcasted_iota(jnp.int32, (nrow, tq), 0)
    pbase = (prow_i >> 3) * grp + (prow_i & 7)

    def top_bits(n):
        n = jnp.clip(n, 0, WORD_BITS)
        return jnp.where(n > 0, lax.shift_left(jnp.int32(-1), (WORD_BITS - n) & (WORD_BITS - 1)), 0)

    def colsum(words):
        c = lax.population_count(words).reshape(nrow // SUBLANES, SUBLANES, tq)
        return jnp.sum(jnp.sum(c, axis=0), axis=0, keepdims=True)

    def bit_body(p, carry):
        cand, above, u = carry
        ones = cand & planes_ref[p]
        c1 = colsum(ones)
        take = (above + c1) >= topk
        cand = jnp.where(take, ones, cand ^ ones)
        above = jnp.where(take, above, above + c1)
        u = u | jnp.where(take, lax.shift_left(jnp.int32(1), WORD_BITS - 1 - p), 0)
        return cand, above, u

    cand0 = top_bits((allowed_end - (prow_i >> 3) * grp) >> 3)
    zero_row = jnp.zeros((1, tq), jnp.int32)
    ties, cnt_gt, u = lax.fori_loop(0, WORD_BITS, bit_body, (cand0, zero_row, zero_row))
    thr = u ^ INT_MIN
    room = topk - cnt_gt

    @pl.when(jnp.max(colsum(ties) - room) > 0)
    def _():
        nbits = max(1, (key_ref.shape[0]).bit_length())

        def tbody(b, c):
            cnd = c | lax.shift_left(jnp.int32(1), nbits - 1 - b)
            cnt = colsum(ties & top_bits((cnd - pbase + 7) >> 3))
            return jnp.where(cnt <= room, cnd, c)

        cut = lax.fori_loop(0, nbits, tbody, zero_row)

        def retire(g, carry):
            start = pl.multiple_of(g * grp, grp)
            keys = key_ref[pl.ds(start, grp), :]
            drop = (keys == thr) & (g * grp + kloc >= cut)
            key_ref[pl.ds(start, grp), :] = jnp.where(drop, INT_MIN, keys)
            return carry

        lax.fori_loop(0, n_grp, retire, 0)

    thr_open = jnp.where(thr > INT_MIN, thr - 1, thr)

    acc_ref[...] = jnp.zeros(acc_ref.shape, f32)
    pb_ref[...] = jnp.zeros(pb_ref.shape, pb_ref.dtype)
    qrow_f = qrow.astype(f32)

    def issue_logits(g, dst_ref):
        start = pl.multiple_of(g * grp, grp)
        for pr in range(C_HEADS // 2):
            dst_ref[pr] = _dot(ck_ref[pl.ds(start, grp), pr * 2 * LANES:(pr + 1) * 2 * LANES], wpr_ref[pr])

    def issue_pv(g, p_ref):
        vt = cvt_ref[g]
        return [_dot(vt[hh * C_VT_ROWS:(hh + 1) * C_VT_ROWS], p_ref[hh]) for hh in range(C_HEADS)]

    def step(g, s_cur, s_nxt, p_cur, p_prev, exact_bias, state):
        pvs = issue_pv(jnp.maximum(g - 1, 0), p_prev)
        start = pl.multiple_of(g * grp, grp)
        sel = key_ref[pl.ds(start, grp), :] > thr_open
        if exact_bias:
            dist = (jnp.abs(qpos - (g * grp + kloc)) + kloc).astype(f32)
        else:
            base = (g * grp).astype(f32) - qrow_f
        out = []
        for hh in range(C_HEADS):
            c = coefs[hh][2]
            s = s_cur[hh // 2, :, (hh % 2) * tq:(hh % 2 + 1) * tq]
            if exact_bias:
                s = s - c * dist
                coff = jnp.zeros((1, tq), f32)
            else:
                coff = c * base
            s = jnp.where(sel, s, NEG_BIG)
            m_new, alpha, p = _softmax_step(s, coff, state[2 * hh])
            p_cur[hh] = p
            out += [m_new, alpha]
        if s_nxt is not None:
            issue_logits(g + 1, s_nxt)
        for hh in range(C_HEADS):
            acc_ref[hh] = state[2 * hh + 1] * acc_ref[hh] + pvs[hh]
        return tuple(out)

    def last_group(s_cur, p_cur, p_prev, state):
        state = step(last, s_cur, None, p_cur, p_prev, True, state)
        pvs = issue_pv(last, p_cur)
        outs = []
        for hh in range(C_HEADS):
            n = state[2 * hh + 1] * acc_ref[hh] + pvs[hh]
            outs.append(n[0:C_HEAD_DIM] / n[C_HEAD_DIM:C_HEAD_DIM + 1])
        return jnp.concatenate(outs, axis=0)

    def pair(t, state):
        state = step(2 * t, sa_ref, sb_ref, pa_ref, pb_ref, False, state)
        return step(2 * t + 1, sb_ref, sa_ref, pb_ref, pa_ref, False, state)

    def odd_tail(state):
        state = step(last - 1, sa_ref, sb_ref, pa_ref, pb_ref, False, state)
        return last_group(sb_ref, pb_ref, pa_ref, state)

    m0 = jnp.full((1, tq), NEG_BIG, f32)
    one = jnp.ones((1, tq), f32)
    issue_logits(0, sa_ref)
    state = lax.fori_loop(0, last // 2, pair, (m0, one) * C_HEADS)
    o_t = lax.cond(last % 2 == 1, odd_tail, lambda st: last_group(sa_ref, pa_ref, pb_ref, st), state)
    o_ref[...] = o_t.T.astype(o_ref.dtype)


def _dsa_attn(hb, cvt, coefs):
    bsz, s, _ = hb.shape
    tq = min(DSA_TQ, s)
    grp = KEY_BLOCK
    topk = min(TOPK_MAX, s // 4)
    cqb, iqb = HB_CQ // C_WIDTH, HB_IQ // C_WIDTH
    ckb, ikwb = HB_CK // (2 * C_WIDTH), HB_IKW // LANES
    kern = functools.partial(_dsa_kernel, topk=topk, tq=tq, coefs=coefs)
    return pl.pallas_call(
        kern,
        grid=(bsz, s // tq),
        in_specs=[
            pl.BlockSpec((None, tq, C_WIDTH), lambda b, i: (b, i, cqb)),
            pl.BlockSpec((None, tq, C_WIDTH), lambda b, i: (b, i, iqb)),
            pl.BlockSpec((None, tq, LANES), lambda b, i: (b, i, ikwb)),
            pl.BlockSpec((None, s, 2 * C_WIDTH), lambda b, i: (b, 0, ckb)),
            pl.BlockSpec((None, s // grp, C_HEADS * C_VT_ROWS, grp), lambda b, i: (b, 0, 0, 0)),
            pl.BlockSpec((None, s, LANES), lambda b, i: (b, 0, ikwb)),
        ],
        out_specs=pl.BlockSpec((None, tq, C_WIDTH), lambda b, i: (b, i, 0)),
        out_shape=jax.ShapeDtypeStruct((bsz, s, C_WIDTH), MXU_DTYPE),
        scratch_shapes=[
            pltpu.VMEM((s, tq), jnp.int32),
            pltpu.VMEM((WORD_BITS, (s // grp) * SUBLANES, tq), jnp.int32),
            pltpu.VMEM((LANES, IDX_HEADS * tq), MXU_DTYPE),
            pltpu.VMEM((C_HEADS // 2, 2 * LANES, 2 * tq), MXU_DTYPE),
            pltpu.VMEM((C_HEADS, C_VT_ROWS, tq), jnp.float32),
            pltpu.VMEM((C_HEADS // 2, grp, 2 * tq), jnp.float32),
            pltpu.VMEM((C_HEADS // 2, grp, 2 * tq), jnp.float32),
            pltpu.VMEM((C_HEADS, grp, tq), MXU_DTYPE),
            pltpu.VMEM((C_HEADS, grp, tq), MXU_DTYPE),
        ],
        compiler_params=pltpu.CompilerParams(
            dimension_semantics=("parallel", "arbitrary"), vmem_limit_bytes=VMEM_LIMIT),
        name="dsa_attn",
    )(hb, hb, hb, hb, cvt, hb)


def _mix_ffn_kernel(oa_ref, ob_ref, oc_ref, x_ref, wo_ref, g1_ref, b1_ref, wg_ref, wu_ref, wd_ref,
                    g2_ref, b2_ref, o_ref, x1_ref, acc_ref, *, alpha, fc):
    half = oa_ref.shape[0] // 2
    for r in range(2):
        rows = slice(r * half, (r + 1) * half)
        y = _dot(oa_ref[rows, :], wo_ref[0:A_WIDTH, :])
        y = y + _dot(ob_ref[rows, :], wo_ref[A_WIDTH:A_WIDTH + B_WIDTH, :])
        y = y + _dot(oc_ref[rows, :], wo_ref[A_WIDTH + B_WIDTH:, :])
        z = alpha * x_ref[rows, :] + y
        x1_ref[rows, :] = _layer_norm_rows(z, g1_ref[...], b1_ref[...])
    xb = x1_ref[...].astype(MXU_DTYPE)
    hidden = wg_ref.shape[1]
    for c0 in range(0, hidden, fc):
        gate = _dot(xb, wg_ref[:, c0:c0 + fc])
        up = _dot(xb, wu_ref[:, c0:c0 + fc])
        hid = (jax.nn.silu(gate) * up).astype(MXU_DTYPE)
        part = _dot(hid, wd_ref[c0:c0 + fc, :])
        if c0 == 0:
            acc_ref[...] = part
        else:
            acc_ref[...] += part
    z = alpha * x1_ref[...] + acc_ref[...]
    o_ref[...] = _layer_norm_rows(z, g2_ref[...], b2_ref[...])


def _mix_ffn(oa, ob, oc, x, wo, g1, b1, wg, wu, wd, g2, b2, alpha):
    bsz, s, d = x.shape
    tm = min(ROW_TILE, s)
    row = lambda width: pl.BlockSpec((None, tm, width), lambda bb, i: (bb, i, 0))
    full = lambda a: pl.BlockSpec(a.shape, lambda bb, i: (0, 0), pipeline_mode=pl.Buffered(1))
    return pl.pallas_call(
        functools.partial(_mix_ffn_kernel, alpha=alpha, fc=256),
        grid=(bsz, s // tm),
        in_specs=[row(A_WIDTH), row(B_WIDTH), row(C_WIDTH), row(d), full(wo), full(g1), full(b1),
                  full(wg), full(wu), full(wd), full(g2), full(b2)],
        out_specs=row(d),
        out_shape=jax.ShapeDtypeStruct((bsz, s, d), jnp.float32),
        scratch_shapes=[pltpu.VMEM((tm, d), jnp.float32), pltpu.VMEM((tm, d), jnp.float32)],
        compiler_params=pltpu.CompilerParams(
            dimension_semantics=("parallel", "parallel"), vmem_limit_bytes=VMEM_LIMIT),
        name="out_proj_ffn",
    )(oa, ob, oc, x, wo, g1, b1, wg, wu, wd, g2, b2)


def _prep_in_weights(w):
    d = w.shape[0]
    off_bq = 2 * A_WIDTH
    off_bk = off_bq + B_WIDTH
    off_bv = off_bk + B_WIDTH
    off_cq = off_bv + B_WIDTH
    off_ck = off_cq + C_WIDTH
    off_cv = off_ck + C_WIDTH
    off_iq = off_cv + C_WIDTH
    off_ik = off_iq + IDX_HEADS * IDX_DIM
    off_iw = off_ik + IDX_DIM
    q_scale = (B_QK_DIM ** -0.5) * LOG2E
    c_scale = (C_HEAD_DIM ** -0.5) * LOG2E
    i_scale = (IDX_HEADS ** -0.5) * (IDX_DIM ** -0.5)
    cols = [w[:, off_bq:off_bk] * q_scale, w[:, off_bk:off_bv], w[:, off_ck:off_cv],
            w[:, off_cq:off_ck] * c_scale, w[:, off_iq:off_iw],
            w[:, off_iw:off_iw + IDX_HEADS] * i_scale, jnp.zeros((d, LANES - IDX_DIM - IDX_HEADS), w.dtype)]
    wb = jnp.concatenate(cols, axis=1)
    wv = jnp.concatenate([w[:, off_bv:off_cq], w[:, off_cv:off_iq]], axis=1)
    return w[:, :off_bq].astype(MXU_DTYPE), wb.astype(MXU_DTYPE), wv.astype(MXU_DTYPE)


def kernel(x, w_in, gmlp_w_s, gmlp_b_s, gmlp_ln_g, gmlp_ln_b, lam_q1, lam_k1, lam_q2, lam_k2,
           diff_subln_g, w_out, ln1_g, ln1_b, w_gu, w_down, ln2_g, ln2_b):
    depth = w_in.shape[0]
    alpha = (2 * depth) ** 0.25
    hidden = w_down.shape[1]
    coefs_b = _alibi_coefs(B_HEADS)
    coefs_c = tuple(_alibi_coefs(C_HEADS))
    coef_tab = jnp.asarray([v for hi, lo, c in coefs_b for v in (hi, lo, c, 0.0)], jnp.float32)
    for l in range(depth):
        lam_init = 0.8 - 0.6 * math.exp(-0.3 * l)
        wa, wb, wv = _prep_in_weights(w_in[l])
        out_a, hb, bvt, cvt = _in_proj(x, wa, wb, wv, gmlp_w_s[l], jnp.repeat(gmlp_b_s[l].T, A_GDIM, axis=1),
                                       gmlp_ln_g[l].reshape(1, A_WIDTH), gmlp_ln_b[l].reshape(1, A_WIDTH))
        lam_p = jnp.stack([lam_q1[l], lam_k1[l], lam_q2[l], lam_k2[l]]).astype(jnp.float32)
        out_b = _diff_attn(hb, bvt, coef_tab, lam_p, diff_subln_g[l].reshape(1, B_V_DIM), lam_init)
        out_c = _dsa_attn(hb, cvt, coefs_c)
        x = _mix_ffn(out_a, out_b, out_c, x, w_out[l].astype(MXU_DTYPE),
                     ln1_g[l].reshape(1, -1), ln1_b[l].reshape(1, -1),
                     w_gu[l, :, :hidden].astype(MXU_DTYPE), w_gu[l, :, hidden:].astype(MXU_DTYPE),
                     w_down[l].astype(MXU_DTYPE), ln2_g[l].reshape(1, -1), ln2_b[l].reshape(1, -1), alpha)
    return x
```
